```python
import jax, jax.numpy as jnp
from jax import lax
import numpy as np

D_MODEL = 1024
BATCH = 8
SEQ = 2048
DEPTH = 2

GRID_W = 64
CTX_LEN = 256
GLA_HEADS = 4
GLA_DK = 128
GLA_DV = 256
GLA_KW = GLA_HEADS * GLA_DK
GLA_VW = GLA_HEADS * GLA_DV
GLA_LOWRANK = 16
GLA_TAU = 16.0
GLA_CHUNK = 64
RNN_WIDTH = 1024
RNN_BLOCKS = 8
RNN_BLOCK = RNN_WIDTH // RNN_BLOCKS
RNN_CONV = 4
RGLRU_C = 8.0
FFN_HIDDEN = 2816
FFN_CONV = 3
NORM_EPS = 1e-6
IN_SIZES = (GLA_KW, GLA_KW, GLA_VW, GLA_VW, 2 * GLA_LOWRANK, RNN_WIDTH, RNN_WIDTH, D_MODEL, D_MODEL)
D_IN = GLA_KW * 2 + GLA_VW * 2 + 2 * GLA_LOWRANK + RNN_WIDTH * 2 + D_MODEL * 2

kernel_name = "hybrid_gla_rglru_convffn_dit"


def rmsnorm(x, w):
    xf = x.astype(jnp.float32)
    y = xf * lax.rsqrt(jnp.mean(xf * xf, axis=-1, keepdims=True) + NORM_EPS)
    return y * w.astype(jnp.float32)


def modulate(h, shift, scale):
    return h * (1.0 + scale) + shift


def split_columns(z):
    idx = np.cumsum(IN_SIZES)[:-1].tolist()
    return jnp.split(z, idx, axis=-1)


def flip(t):
    return jnp.flip(t, axis=1)


def dwconv1d(x, w, bias):
    K, ch = w.shape
    pl = (K - 1) // 2
    y = lax.conv_general_dilated(
        x, w[:, None, :].astype(x.dtype), window_strides=(1,),
        padding=[(pl, K - 1 - pl)], dimension_numbers=("NWC", "WIO", "NWC"),
        feature_group_count=ch)
    return y + bias


def dwconv2d_grid(x, w, bias):
    B, T, ch = x.shape
    rows = T // GRID_W
    xg = x.reshape(B, rows, GRID_W, ch)
    y = lax.conv_general_dilated(
        xg, w[:, :, None, :].astype(x.dtype), window_strides=(1, 1),
        padding=[(1, 1), (1, 1)], dimension_numbers=("NHWC", "HWIO", "NHWC"),
        feature_group_count=ch)
    return y.reshape(B, T, ch) + bias


def gla_chunked(q, k, v, log_a, s0, with_output):
    B, T, H, K = k.shape
    V = v.shape[-1]
    C = GLA_CHUNK
    N = T // C
    f32 = jnp.float32
    k = k.astype(f32).reshape(B, N, C, H, K)
    v = v.astype(f32).reshape(B, N, C, H, V)
    b = jnp.cumsum(log_a.astype(f32).reshape(B, N, C, H, K), axis=2)
    b_last = b[:, :, -1]
    k_dec = k * jnp.exp(b_last[:, :, None] - b)
    cf = lambda t: jnp.moveaxis(t, 1, 0)

    def update(S, kd, vc, bl):
        return S * jnp.exp(bl)[..., None] + jnp.einsum("bchk,bchv->bhkv", kd, vc)

    if not with_output:
        def step_state(S, xs):
            kd, vc, bl = xs
            return update(S, kd, vc, bl), None
        s_fin, _ = lax.scan(step_state, s0, (cf(k_dec), cf(v), cf(b_last)))
        return None, s_fin

    q = q.astype(f32).reshape(B, N, C, H, K) * (K ** -0.5)
    q_dec = q * jnp.exp(b)
    k_inv = k * jnp.exp(-b)
    causal_in_chunk = jnp.tril(jnp.ones((C, C), dtype=bool))
    scores = jnp.einsum("bnchk,bnshk->bnhcs", q_dec, k_inv)
    scores = jnp.where(causal_in_chunk, scores, 0.0)
    o_intra = jnp.einsum("bnhcs,bnshv->bnchv", scores, v)

    def step(S, xs):
        qd, kd, vc, bl = xs
        o = jnp.einsum("bchk,bhkv->bchv", qd, S)
        return update(S, kd, vc, bl), o

    s_fin, o_inter = lax.scan(step, s0, (cf(q_dec), cf(k_dec), cf(v), cf(b_last)))
    o = o_intra + jnp.moveaxis(o_inter, 0, 1)
    return o.reshape(B, T, H, V), s_fin


def linear_scan(a, u, h0):
    u = u.at[:, 0].add(a[:, 0] * h0)

    def combine(left, right):
        return left[0] * right[0], right[0] * left[1] + right[1]

    _, h = lax.associative_scan(combine, (a, u), axis=1)
    return h


def rglru(x, wa, ba, wx, bx, lam, h0):
    B, T, R = x.shape
    f32 = jnp.float32
    x = x.astype(f32)
    xb = x.reshape(B, T, RNN_BLOCKS, RNN_BLOCK)

    def gate(w, bias):
        return jax.nn.sigmoid(jnp.einsum("btgi,gij->btgj", xb, w.astype(f32)).reshape(B, T, R) + bias.astype(f32))

    r = gate(wa, ba)
    i = gate(wx, bx)
    log_a = RGLRU_C * r * jax.nn.log_sigmoid(lam.astype(f32))
    a = jnp.exp(log_a)
    u = x * i * jnp.sqrt(-jnp.expm1(2.0 * log_a))
    h = linear_scan(a, u, h0)
    return h, h[:, -1]


def mix_stream(z, gla_s0, rnn_h0, lp, with_output):
    q, k, v, g, lr, xr, yr, gate_a, gate_b = split_columns(z)
    B, T = z.shape[:2]
    heads = lambda t, d: t.reshape(B, T, GLA_HEADS, d)
    lr_f, lr_b = jnp.split(lr, 2, axis=-1)
    la_f = jax.nn.log_sigmoid(lr_f @ lp["gla_lr_w"][0] + lp["gla_lr_b"][0]) / GLA_TAU
    la_b = jax.nn.log_sigmoid(lr_b @ lp["gla_lr_w"][1] + lp["gla_lr_b"][1]) / GLA_TAU
    qh, kh, vh = heads(q, GLA_DK), heads(k, GLA_DK), heads(v, GLA_DV)
    o_f, s_f = gla_chunked(qh, kh, vh, heads(la_f, GLA_DK), gla_s0[0], with_output)
    o_b, s_b = gla_chunked(flip(qh), flip(kh), flip(vh), flip(heads(la_b, GLA_DK)), gla_s0[1], with_output)

    xc = dwconv1d(xr, lp["rnn_conv_w"], lp["rnn_conv_b"])
    h_f, hf_last = rglru(xc, lp["rnn_wa"][0], lp["rnn_ba"][0], lp["rnn_wx"][0], lp["rnn_bx"][0],
                         lp["rnn_lambda"][0], rnn_h0[0])
    h_b, hb_last = rglru(flip(xc), lp["rnn_wa"][1], lp["rnn_ba"][1], lp["rnn_wx"][1], lp["rnn_bx"][1],
                         lp["rnn_lambda"][1], rnn_h0[1])
    if not with_output:
        return None, (s_f, s_b), (hf_last, hb_last)

    o = o_f + flip(o_b)
    o = o * lax.rsqrt(jnp.mean(o * o, axis=-1, keepdims=True) + NORM_EPS)
    o = o.reshape(B, T, GLA_VW) * lp["gla_norm_w"] * jax.nn.silu(g)
    r = (h_f + flip(h_b)) * jax.nn.gelu(yr)
    merged = (jax.nn.sigmoid(gate_a) * (o @ lp["w_gla_o"])
              + jax.nn.sigmoid(gate_b) * (r @ lp["w_rnn_o"]))
    return merged @ lp["w_out"], (s_f, s_b), (hf_last, hb_last)


def conv_ffn(h, w_up, conv_w, conv_b, w_down, on_grid):
    u = h @ w_up
    a, gv = jnp.split(u, 2, axis=-1)
    a = dwconv2d_grid(a, conv_w, conv_b) if on_grid else dwconv1d(a, conv_w[1], conv_b)
    return (jax.nn.gelu(a) * gv) @ w_down


def setup_inputs(seed: int = 0) -> dict:
    key = jax.random.key(seed)
    ks = jax.random.split(key, 32)
    f32 = jnp.float32
    nrm = lambda k, shape, s: jax.random.normal(k, shape, f32) * s
    L = DEPTH
    a8 = jax.random.uniform(ks[17], (L, 2, RNN_WIDTH), f32, 0.9, 0.999)
    p = a8 ** (1.0 / RGLRU_C)
    return {
        "x": nrm(ks[0], (BATCH, SEQ, D_MODEL), 1.0),
        "c": nrm(ks[1], (BATCH, D_MODEL), 1.0),
        "ctx": nrm(ks[2], (BATCH, CTX_LEN, D_MODEL), 1.0),
        "c_ctx": nrm(ks[3], (D_MODEL,), 1.0),
        "ada_w": nrm(ks[4], (L, D_MODEL, 6 * D_MODEL), 0.5 * D_MODEL ** -0.5),
        "ada_b": nrm(ks[5], (L, 6 * D_MODEL), 0.01),
        "norm1_w": 1.0 + nrm(ks[6], (L, D_MODEL), 0.02),
        "w_in": nrm(ks[7], (L, D_MODEL, D_IN), D_MODEL ** -0.5),
        "gla_lr_w": nrm(ks[8], (L, 2, GLA_LOWRANK, GLA_KW), GLA_LOWRANK ** -0.5),
        "gla_lr_b": nrm(ks[9], (L, 2, GLA_KW), 0.1),
        "gla_norm_w": 1.0 + nrm(ks[10], (L, GLA_VW), 0.02),
        "rnn_conv_w": nrm(ks[11], (L, RNN_CONV, RNN_WIDTH), RNN_CONV ** -0.5),
        "rnn_conv_b": nrm(ks[12], (L, RNN_WIDTH), 0.01),
        "rnn_wa": nrm(ks[13], (L, 2, RNN_BLOCKS, RNN_BLOCK, RNN_BLOCK), RNN_BLOCK ** -0.5),
        "rnn_ba": nrm(ks[14], (L, 2, RNN_WIDTH), 0.01),
        "rnn_wx": nrm(ks[15], (L, 2, RNN_BLOCKS, RNN_BLOCK, RNN_BLOCK), RNN_BLOCK ** -0.5),
        "rnn_bx": nrm(ks[16], (L, 2, RNN_WIDTH), 0.01),
        "rnn_lambda": jnp.log(p) - jnp.log1p(-p),
        "w_gla_o": nrm(ks[18], (L, GLA_VW, D_MODEL), GLA_VW ** -0.5),
        "w_rnn_o": nrm(ks[19], (L, RNN_WIDTH, D_MODEL), RNN_WIDTH ** -0.5),
        "w_out": nrm(ks[20], (L, D_MODEL, D_MODEL), D_MODEL ** -0.5),
        "norm2_w": 1.0 + nrm(ks[21], (L, D_MODEL), 0.02),
        "ffn_up": nrm(ks[22], (L, D_MODEL, 2 * FFN_HIDDEN), D_MODEL ** -0.5),
        "ffn_conv_w": nrm(ks[23], (L, FFN_CONV, FFN_CONV, FFN_HIDDEN), 1.0 / FFN_CONV),
        "ffn_conv_b": nrm(ks[24], (L, FFN_HIDDEN), 0.01),
        "ffn_down": nrm(ks[25], (L, FFN_HIDDEN, D_MODEL), FFN_HIDDEN ** -0.5),
        "final_norm_w": 1.0 + nrm(ks[26], (D_MODEL,), 0.02),
    }


def reference(x, c, ctx, c_ctx, ada_w, ada_b, norm1_w, w_in, gla_lr_w, gla_lr_b, gla_norm_w,
              rnn_conv_w, rnn_conv_b, rnn_wa, rnn_ba, rnn_wx, rnn_bx, rnn_lambda,
              w_gla_o, w_rnn_o, w_out, norm2_w, ffn_up, ffn_conv_w, ffn_conv_b, ffn_down,
              final_norm_w):
    B = x.shape[0]
    f32 = jnp.float32
    gla_zero = jnp.zeros((B, GLA_HEADS, GLA_DK, GLA_DV), f32)
    rnn_zero = jnp.zeros((B, RNN_WIDTH), f32)
    for l in range(DEPTH):
        lp = dict(gla_lr_w=gla_lr_w[l], gla_lr_b=gla_lr_b[l], gla_norm_w=gla_norm_w[l],
                  rnn_conv_w=rnn_conv_w[l], rnn_conv_b=rnn_conv_b[l],
                  rnn_wa=rnn_wa[l], rnn_ba=rnn_ba[l], rnn_wx=rnn_wx[l], rnn_bx=rnn_bx[l],
                  rnn_lambda=rnn_lambda[l], w_gla_o=w_gla_o[l], w_rnn_o=w_rnn_o[l], w_out=w_out[l])
        last = l == DEPTH - 1
        mod_x = (jax.nn.silu(c) @ ada_w[l] + ada_b[l])[:, None, :]
        mod_c = jax.nn.silu(c_ctx) @ ada_w[l] + ada_b[l]
        sh1, sc1, g1, sh2, sc2, g2 = jnp.split(mod_x, 6, axis=-1)
        csh1, csc1, cg1, csh2, csc2, cg2 = jnp.split(mod_c, 6, axis=-1)

        hc = modulate(rmsnorm(ctx, norm1_w[l]), csh1, csc1)
        hx = modulate(rmsnorm(x, norm1_w[l]), sh1, sc1)
        mc, gla_states, rnn_states = mix_stream(hc @ w_in[l], (gla_zero, gla_zero),
                                                (rnn_zero, rnn_zero), lp, not last)
        mx, _, _ = mix_stream(hx @ w_in[l], gla_states, rnn_states, lp, True)
        x = (x + g1 * mx).astype(x.dtype)

        hx2 = modulate(rmsnorm(x, norm2_w[l]), sh2, sc2)
        x = (x + g2 * conv_ffn(hx2, ffn_up[l], ffn_conv_w[l], ffn_conv_b[l], ffn_down[l], True)).astype(x.dtype)

        if not last:
            ctx = (ctx + cg1 * mc).astype(ctx.dtype)
            hc2 = modulate(rmsnorm(ctx, norm2_w[l]), csh2, csc2)
            ctx = (ctx + cg2 * conv_ffn(hc2, ffn_up[l], ffn_conv_w[l], ffn_conv_b[l], ffn_down[l], False)).astype(ctx.dtype)
    return rmsnorm(x, final_norm_w).astype(x.dtype)
```

```python
import functools

import jax
import jax.numpy as jnp
from jax import lax
from jax.experimental import pallas as pl
from jax.experimental.pallas import tpu as pltpu

F32 = jnp.float32
BF16 = jnp.bfloat16

D_MODEL = 1024
GRID_W = 64
GLA_HEADS = 4
GLA_DK = 128
GLA_DV = 256
GLA_KW = GLA_HEADS * GLA_DK
GLA_VW = GLA_HEADS * GLA_DV
GLA_LOWRANK = 16
GLA_TAU = 16.0
GLA_CHUNK = 64
RNN_WIDTH = 1024
RNN_BLOCKS = 8
RNN_BLOCK = RNN_WIDTH // RNN_BLOCKS
RGLRU_C = 8.0
FFN_HIDDEN = 2816
NORM_EPS = 1e-6

LANES = 128
SCAN_SEGS = 8
VMEM_LIMIT = 56 * 1024 * 1024

Z_Q, Z_K, Z_V, Z_G, Z_XR, Z_YR, Z_GA, Z_GB = 0, 512, 1024, 2048, 3072, 4096, 5120, 6144
Z_WIDTH = 7168
LR_START = 2 * GLA_KW + 2 * GLA_VW
PROJ_TM = 1024
PROJ_TN = 512
MIX_TM = 512
FFN_TM = 512
FFN_CK = 128


def _cparams(sem):
    return pltpu.CompilerParams(dimension_semantics=sem, vmem_limit_bytes=VMEM_LIMIT)


def _sigmoid(x):
    return 1.0 / (1.0 + jnp.exp(-x))


def _silu(x):
    return x * _sigmoid(x)


def _gelu_tanh(x):
    return 0.5 * x * (1.0 + jnp.tanh(0.7978845608028654 * (x + 0.044715 * (x * x * x))))


def _log_sigmoid(x):
    return jnp.minimum(x, 0.0) - jnp.log1p(jnp.exp(-jnp.abs(x)))


def _identity(x):
    return x


def _ada_kernel(c_ref, w_ref, b_ref, o_ref):
    s = _silu(c_ref[...])
    o_ref[0] = jnp.dot(s.astype(BF16), w_ref[0].astype(BF16), preferred_element_type=F32) + b_ref[0]


def _ada(cc, ada_w, ada_b):
    L, D, N = ada_w.shape
    tn = 1536
    return pl.pallas_call(
        _ada_kernel,
        grid=(L, N // tn),
        in_specs=[
            pl.BlockSpec((16, D), lambda l, n: (0, 0)),
            pl.BlockSpec((1, D, tn), lambda l, n: (l, 0, n)),
            pl.BlockSpec((1, 1, tn), lambda l, n: (l, 0, n)),
        ],
        out_specs=pl.BlockSpec((1, 16, tn), lambda l, n: (l, 0, n)),
        out_shape=jax.ShapeDtypeStruct((L, 16, N), F32),
        name="ada_mod",
        compiler_params=_cparams(("arbitrary", "arbitrary")),
    )(cc, ada_w, ada_b.reshape(L, 1, N))


def _proj_kernel(*refs, acts, has_lr):
    if has_lr:
        x_ref, nw_ref, sh_ref, sc_ref, w_ref, wlr_ref, o_ref, olr_ref, xn_scr = refs
    else:
        x_ref, nw_ref, sh_ref, sc_ref, w_ref, o_ref, xn_scr = refs
    n = pl.program_id(2)

    @pl.when(n == 0)
    def _():
        x = x_ref[0]
        ms = jnp.mean(x * x, axis=-1, keepdims=True)
        y = x * lax.rsqrt(ms + NORM_EPS) * nw_ref[...]
        y = y * (1.0 + sc_ref[0]) + sh_ref[0]
        xb = y.astype(BF16)
        xn_scr[...] = xb
        if has_lr:
            olr_ref[0] = jnp.dot(xb, wlr_ref[...], preferred_element_type=F32).astype(olr_ref.dtype)

    acc = jnp.dot(xn_scr[...], w_ref[...], preferred_element_type=F32)
    for lo, hi, fn in acts:
        @pl.when((n >= lo) & (n < hi))
        def _(fn=fn):
            o_ref[0] = fn(acc).astype(o_ref.dtype)


def _proj(x, norm_w, shift, scale, w, wlr, acts):
    Bx, Tx, D = x.shape
    N = w.shape[1]
    tm = min(PROJ_TM, Tx)
    tn = PROJ_TN
    per_batch = shift.shape[0] > 1
    mod_map = (lambda b, i, n: (b, 0, 0)) if per_batch else (lambda b, i, n: (0, 0, 0))
    has_lr = wlr is not None
    in_specs = [
        pl.BlockSpec((1, tm, D), lambda b, i, n: (b, i, 0)),
        pl.BlockSpec((1, D), lambda b, i, n: (0, 0)),
        pl.BlockSpec((1, 1, D), mod_map),
        pl.BlockSpec((1, 1, D), mod_map),
        pl.BlockSpec((D, tn), lambda b, i, n: (0, n)),
    ]
    args = [x, norm_w.reshape(1, D), shift, scale, w]
    out_specs = [pl.BlockSpec((1, tm, tn), lambda b, i, n: (b, i, n))]
    out_shape = [jax.ShapeDtypeStruct((Bx, Tx, N), BF16)]
    if has_lr:
        in_specs.append(pl.BlockSpec((D, LANES), lambda b, i, n: (0, 0)))
        args.append(wlr)
        out_specs.append(pl.BlockSpec((1, tm, LANES), lambda b, i, n: (b, i, 0)))
        out_shape.append(jax.ShapeDtypeStruct((Bx, Tx, LANES), BF16))
    res = pl.pallas_call(
        functools.partial(_proj_kernel, acts=acts, has_lr=has_lr),
        grid=(Bx, Tx // tm, N // tn),
        in_specs=in_specs,
        out_specs=out_specs,
        out_shape=out_shape,
        scratch_shapes=[pltpu.VMEM((tm, D), BF16)],
        name="norm_mod_proj",
        compiler_params=_cparams(("arbitrary", "arbitrary", "arbitrary")),
    )(*args)
    return res if has_lr else res[0]


_IN_ACTS = (
    (0, Z_G // PROJ_TN, _identity),
    (Z_G // PROJ_TN, Z_XR // PROJ_TN, _silu),
    (Z_XR // PROJ_TN, Z_YR // PROJ_TN, _identity),
    (Z_YR // PROJ_TN, Z_GA // PROJ_TN, _gelu_tanh),
    (Z_GA // PROJ_TN, Z_WIDTH // PROJ_TN, _sigmoid),
)
_UP_ACTS = ((0, 2 * FFN_HIDDEN // PROJ_TN, _identity),)


_NT = (((1,), (1,)), ((), ()))
_TN = (((0,), (0,)), ((), ()))


def _gla_kernel(q_ref, k_ref, v_ref, g_ref, lr_ref, qc_ref, kc_ref, vc_ref, gc_ref, lrc_ref,
                lrw_ref, lrb_ref, nw_ref, o_ref, oc_ref, la_scr, oacc_scr, s_scr, *, ctx_out):
    C = GLA_CHUNK
    row = lax.broadcasted_iota(jnp.int32, (C, C), 0)
    col = lax.broadcasted_iota(jnp.int32, (C, C), 1)
    masks = (col <= row, col >= row)
    tris = tuple(jnp.where(m, 1.0, 0.0).astype(BF16) for m in masks)
    qscale = GLA_DK ** -0.5

    def fill_la(lr_r, T):
        lrv = lr_r[0]
        for d in range(2):
            xx = jnp.dot(lrv, lrw_ref[d], preferred_element_type=F32) + lrb_ref[d:d + 1, :]
            la_scr[d, 0:T, :] = _log_sigmoid(xx) * (1.0 / GLA_TAU)

    def run(q_r, k_r, v_r, T, d, with_out):
        N = T // C
        mask, tri = masks[d], tris[d]

        def body(n, carry):
            blk = n if d == 0 else N - 1 - n
            r0 = pl.multiple_of(blk * C, C)
            la = la_scr[d, pl.ds(r0, C), :]
            hi = la.astype(BF16)
            lo = (la - hi.astype(F32)).astype(BF16)
            b = (jnp.dot(tri, hi, preferred_element_type=F32)
                 + jnp.dot(tri, lo, preferred_element_type=F32))
            bl = b[C - 1:C, :] if d == 0 else b[0:1, :]
            kf = k_r[0, pl.ds(r0, C), :].astype(F32)
            vb = v_r[0, pl.ds(r0, C), :]
            kd = (kf * jnp.exp(bl - b)).astype(BF16)
            st = s_scr[d]
            if with_out:
                qf = q_r[0, pl.ds(r0, C), :].astype(F32) * qscale
                qd = (qf * jnp.exp(b)).astype(BF16)
                ki = (kf * jnp.exp(-b)).astype(BF16)
                sc = lax.dot_general(qd, ki, _NT, preferred_element_type=F32)
                sc = jnp.where(mask, sc, 0.0)
                o = (jnp.dot(sc.astype(BF16), vb, preferred_element_type=F32)
                     + lax.dot_general(qd, st.astype(BF16), _NT, preferred_element_type=F32))
                if d == 0:
                    oacc_scr[pl.ds(r0, C), :] = o
                else:
                    oacc_scr[pl.ds(r0, C), :] = oacc_scr[pl.ds(r0, C), :] + o
            s_scr[d] = st * jnp.exp(bl) + lax.dot_general(vb, kd, _TN, preferred_element_type=F32)
            return carry

        lax.fori_loop(0, N, body, 0)

    def finish(g_r, out_r, T):
        o = oacc_scr[0:T, :]
        inv = lax.rsqrt(jnp.mean(o * o, axis=-1, keepdims=True) + NORM_EPS)
        out_r[0] = (o * inv * nw_ref[...] * g_r[0].astype(F32)).astype(out_r.dtype)

    Tc = qc_ref.shape[1]
    T = q_ref.shape[1]
    s_scr[...] = jnp.zeros_like(s_scr)
    fill_la(lrc_ref, Tc)
    run(qc_ref, kc_ref, vc_ref, Tc, 0, ctx_out)
    run(qc_ref, kc_ref, vc_ref, Tc, 1, ctx_out)
    if ctx_out:
        finish(gc_ref, oc_ref, Tc)
    else:
        oc_ref[...] = jnp.zeros_like(oc_ref)
    fill_la(lr_ref, T)
    run(q_ref, k_ref, v_ref, T, 0, True)
    run(q_ref, k_ref, v_ref, T, 1, True)
    finish(g_ref, o_ref, T)


def _gla(z, lr, zc, lrc, lrw, lrb, norm_w, ctx_out):
    B, T, _ = z.shape
    Tc = zc.shape[1]
    H = GLA_HEADS

    def specs(Tx):
        return [
            pl.BlockSpec((1, Tx, GLA_DK), lambda b, h: (b, 0, Z_Q // GLA_DK + h)),
            pl.BlockSpec((1, Tx, GLA_DK), lambda b, h: (b, 0, Z_K // GLA_DK + h)),
            pl.BlockSpec((1, Tx, GLA_DV), lambda b, h: (b, 0, Z_V // GLA_DV + h)),
            pl.BlockSpec((1, Tx, GLA_DV), lambda b, h: (b, 0, Z_G // GLA_DV + h)),
            pl.BlockSpec((1, Tx, LANES), lambda b, h: (b, 0, 0)),
        ]

    in_specs = specs(T) + specs(Tc) + [
        pl.BlockSpec((2, LANES, GLA_DK), lambda b, h: (0, 0, h)),
        pl.BlockSpec((2, GLA_DK), lambda b, h: (0, h)),
        pl.BlockSpec((1, GLA_DV), lambda b, h: (0, h)),
    ]
    return pl.pallas_call(
        functools.partial(_gla_kernel, ctx_out=ctx_out),
        grid=(B, H),
        in_specs=in_specs,
        out_specs=[
            pl.BlockSpec((1, T, GLA_DV), lambda b, h: (b, 0, h)),
            pl.BlockSpec((1, Tc, GLA_DV), lambda b, h: (b, 0, h)),
        ],
        out_shape=[
            jax.ShapeDtypeStruct((B, T, GLA_VW), BF16),
            jax.ShapeDtypeStruct((B, Tc, GLA_VW), BF16),
        ],
        scratch_shapes=[
            pltpu.VMEM((2, T, GLA_DK), F32),
            pltpu.VMEM((T, GLA_DV), F32),
            pltpu.VMEM((2, GLA_DV, GLA_DK), F32),
        ],
        name="gla_bidir",
        compiler_params=_cparams(("arbitrary", "arbitrary")),
    )(z, z, z, z, lr, zc, zc, zc, zc, lrc, lrw, lrb, norm_w.reshape(1, GLA_VW))


def _rglru_kernel(xr_ref, yr_ref, xrc_ref, yrc_ref, cw_ref, cb_ref, wg_ref, bg_ref, lam_ref,
                  r_ref, rc_ref, af_scr, uf_scr, ab_scr, ub_scr, *, ctx_out):
    W = RNN_BLOCK
    a_scrs = (af_scr, ab_scr)
    u_scrs = (uf_scr, ub_scr)
    log_sig_lam = _log_sigmoid(lam_ref[...])

    def stream(x_r, y_r, out_r, h0, with_out):
        T = x_r.shape[1]
        L = T // SCAN_SEGS
        P = L + 8
        x = x_r[0].astype(F32)
        t = lax.broadcasted_iota(jnp.int32, (T, W), 0)
        xm1 = jnp.where(t >= 1, pltpu.roll(x, 1, 0), 0.0)
        xp1 = jnp.where(t < T - 1, pltpu.roll(x, T - 1, 0), 0.0)
        xp2 = jnp.where(t < T - 2, pltpu.roll(x, T - 2, 0), 0.0)
        xc = (cw_ref[0:1, :] * xm1 + cw_ref[1:2, :] * x + cw_ref[2:3, :] * xp1
              + cw_ref[3:4, :] * xp2 + cb_ref[...])
        gates = jnp.dot(xc.astype(BF16), wg_ref[0], preferred_element_type=F32) + bg_ref[0]
        for d in range(2):
            rg = _sigmoid(gates[:, (2 * d) * W:(2 * d + 1) * W])
            ig = _sigmoid(gates[:, (2 * d + 1) * W:(2 * d + 2) * W])
            log_a = RGLRU_C * rg * log_sig_lam[d:d + 1, :]
            a = jnp.exp(log_a)
            th = jnp.tanh(log_a)
            u = xc * ig * jnp.sqrt(-2.0 * th / (1.0 - th))
            for j in range(SCAN_SEGS):
                a_scrs[d][j * P:j * P + L, :] = a[j * L:(j + 1) * L, :]
                u_scrs[d][j * P:j * P + L, :] = u[j * L:(j + 1) * L, :]

        def body(i, carry):
            hf, pf, hb, pb = carry
            rf = pl.ds(i, SCAN_SEGS, stride=P)
            a = af_scr[rf, :]
            hf = a * hf + uf_scr[rf, :]
            pf = pf * a
            uf_scr[rf, :] = hf
            af_scr[rf, :] = pf
            rb = pl.ds(L - 1 - i, SCAN_SEGS, stride=P)
            a = ab_scr[rb, :]
            hb = a * hb + ub_scr[rb, :]
            pb = pb * a
            ub_scr[rb, :] = hb
            ab_scr[rb, :] = pb
            return hf, pf, hb, pb

        zero = jnp.zeros((SCAN_SEGS, W), F32)
        one = jnp.ones((SCAN_SEGS, W), F32)
        hf, pf, hb, pb = lax.fori_loop(0, L, body, (zero, one, zero, one))

        c = h0[0]
        cf = []
        for j in range(SCAN_SEGS):
            cf.append(c)
            c = hf[j:j + 1, :] + pf[j:j + 1, :] * c
        fin_f = c
        c = h0[1]
        cb = [None] * SCAN_SEGS
        for j in reversed(range(SCAN_SEGS)):
            cb[j] = c
            c = hb[j:j + 1, :] + pb[j:j + 1, :] * c
        fin_b = c
        if with_out:
            for j in range(SCAN_SEGS):
                seg = slice(j * P, j * P + L)
                h = (uf_scr[seg, :] + af_scr[seg, :] * cf[j]) + (ub_scr[seg, :] + ab_scr[seg, :] * cb[j])
                out_r[0, j * L:(j + 1) * L, :] = (
                    h * y_r[0, j * L:(j + 1) * L, :].astype(F32)).astype(out_r.dtype)
        return fin_f, fin_b

    zrow = jnp.zeros((1, W), F32)
    states = stream(xrc_ref, yrc_ref, rc_ref, (zrow, zrow), ctx_out)
    if not ctx_out:
        rc_ref[...] = jnp.zeros_like(rc_ref)
    stream(xr_ref, yr_ref, r_ref, states, True)


def _rglru(z, zc, conv_w, conv_b, wg, bg, lam, ctx_out):
    B, T, _ = z.shape
    Tc = zc.shape[1]
    G, W = RNN_BLOCKS, RNN_BLOCK
    xoff, yoff = Z_XR // W, Z_YR // W
    seg_rows = SCAN_SEGS * (T // SCAN_SEGS + 8)
    return pl.pallas_call(
        functools.partial(_rglru_kernel, ctx_out=ctx_out),
        grid=(B, G),
        in_specs=[
            pl.BlockSpec((1, T, W), lambda b, g: (b, 0, xoff + g)),
            pl.BlockSpec((1, T, W), lambda b, g: (b, 0, yoff + g)),
            pl.BlockSpec((1, Tc, W), lambda b, g: (b, 0, xoff + g)),
            pl.BlockSpec((1, Tc, W), lambda b, g: (b, 0, yoff + g)),
            pl.BlockSpec((4, W), lambda b, g: (0, g)),
            pl.BlockSpec((1, W), lambda b, g: (0, g)),
            pl.BlockSpec((1, W, 4 * W), lambda b, g: (g, 0, 0)),
            pl.BlockSpec((1, 1, 4 * W), lambda b, g: (g, 0, 0)),
            pl.BlockSpec((2, W), lambda b, g: (0, g)),
        ],
        out_specs=[
            pl.BlockSpec((1, T, W), lambda b, g: (b, 0, g)),
            pl.BlockSpec((1, Tc, W), lambda b, g: (b, 0, g)),
        ],
        out_shape=[
            jax.ShapeDtypeStruct((B, T, RNN_WIDTH), BF16),
            jax.ShapeDtypeStruct((B, Tc, RNN_WIDTH), BF16),
        ],
        scratch_shapes=[pltpu.VMEM((seg_rows, W), F32) for _ in range(4)],
        name="rglru_bidir",
        compiler_params=_cparams(("arbitrary", "arbitrary")),
    )(z, z, zc, zc, conv_w, conv_b.reshape(1, RNN_WIDTH), wg, bg, lam)


def _mix_kernel(on_ref, r_ref, ga_ref, gb_ref, x_ref, g1_ref, wg_ref, wr_ref, wo_ref, o_ref):
    ya = jnp.dot(on_ref[0], wg_ref[...], preferred_element_type=F32) * ga_ref[0].astype(F32)
    yb = jnp.dot(r_ref[0], wr_ref[...], preferred_element_type=F32) * gb_ref[0].astype(F32)
    y = jnp.dot((ya + yb).astype(BF16), wo_ref[...], preferred_element_type=F32)
    o_ref[0] = x_ref[0] + g1_ref[0] * y


def _mix(on, r, z, x, gate, wg, wr, wo):
    Bx, Tx, D = x.shape
    tm = min(MIX_TM, Tx)
    per_batch = gate.shape[0] > 1
    gmap = (lambda b, i: (b, 0, 0)) if per_batch else (lambda b, i: (0, 0, 0))
    row = lambda b, i: (b, i, 0)
    wmap = lambda b, i: (0, 0)
    return pl.pallas_call(
        _mix_kernel,
        grid=(Bx, Tx // tm),
        in_specs=[
            pl.BlockSpec((1, tm, D), row),
            pl.BlockSpec((1, tm, D), row),
            pl.BlockSpec((1, tm, D), lambda b, i: (b, i, Z_GA // D_MODEL)),
            pl.BlockSpec((1, tm, D), lambda b, i: (b, i, Z_GB // D_MODEL)),
            pl.BlockSpec((1, tm, D), row),
            pl.BlockSpec((1, 1, D), gmap),
            pl.BlockSpec((D, D), wmap),
            pl.BlockSpec((D, D), wmap),
            pl.BlockSpec((D, D), wmap),
        ],
        out_specs=pl.BlockSpec((1, tm, D), row),
        out_shape=jax.ShapeDtypeStruct((Bx, Tx, D), F32),
        name="mix_out",
        compiler_params=_cparams(("arbitrary", "arbitrary")),
    )(on, r, z, z, x, gate, wg, wr, wo)


def _ffn_down_kernel(*refs, grid_w, halo, final_norm):
    if halo:
        a_ref, up_ref, dn_ref, gv_ref = refs[:4]
        rest = refs[4:]
    else:
        a_ref, gv_ref = refs[:2]
        rest = refs[2:]
    if final_norm:
        cw_ref, cb_ref, wd_ref, x_ref, g2_ref, fw_ref, o_ref, h_scr = rest
    else:
        cw_ref, cb_ref, wd_ref, x_ref, g2_ref, o_ref, h_scr = rest
    TM = a_ref.shape[1]
    CK = FFN_CK
    i = pl.program_id(1)
    n_i = pl.num_programs(1)
    ext_rows = TM + 2 * grid_w if halo else TM
    assert grid_w & (grid_w - 1) == 0
    colidx = lax.broadcasted_iota(jnp.int32, (ext_rows, CK), 0) & (grid_w - 1)
    first_col = colidx == 0
    last_col = colidx == grid_w - 1
    if halo:
        up_scale = jnp.where(i > 0, 1.0, 0.0).astype(F32)
        dn_scale = jnp.where(i < n_i - 1, 1.0, 0.0).astype(F32)

    def chunk(c, carry):
        c0 = pl.multiple_of(c * CK, CK)
        cs = pl.ds(c0, CK)
        am = a_ref[0, :, cs].astype(F32)
        if halo:
            ext = jnp.concatenate(
                [up_ref[0, :, cs].astype(F32) * up_scale, am, dn_ref[0, :, cs].astype(F32) * dn_scale], axis=0)
        else:
            ext = am
        left = jnp.where(first_col, 0.0, pltpu.roll(ext, 1, 0))
        right = jnp.where(last_col, 0.0, pltpu.roll(ext, ext_rows - 1, 0))
        acc = cb_ref[:, cs]
        for kh in ((0, 1, 2) if halo else (1,)):
            lo = kh * grid_w if halo else 0
            acc = acc + (cw_ref[3 * kh:3 * kh + 1, cs] * left[lo:lo + TM, :]
                         + cw_ref[3 * kh + 1:3 * kh + 2, cs] * ext[lo:lo + TM, :]
                         + cw_ref[3 * kh + 2:3 * kh + 3, cs] * right[lo:lo + TM, :])
        h_scr[:, cs] = (_gelu_tanh(acc) * gv_ref[0, :, cs].astype(F32)).astype(BF16)
        return carry

    lax.fori_loop(0, FFN_HIDDEN // CK, chunk, 0)
    y = jnp.dot(h_scr[...], wd_ref[...], preferred_element_type=F32)
    xo = x_ref[0] + g2_ref[0] * y
    if final_norm:
        ms = jnp.mean(xo * xo, axis=-1, keepdims=True)
        xo = xo * lax.rsqrt(ms + NORM_EPS) * fw_ref[...]
    o_ref[0] = xo


def _ffn_down(u, x, gate, conv_w9, conv_b, wd, grid_w, final_w):
    Bx, Tx, D = x.shape
    Hd = FFN_HIDDEN
    halo = grid_w < Tx
    tm = min(FFN_TM, Tx)
    rpt = tm // grid_w if halo else 1
    n_rows = Tx // grid_w if halo else 1
    per_batch = gate.shape[0] > 1
    gmap = (lambda b, i: (b, 0, 0)) if per_batch else (lambda b, i: (0, 0, 0))
    row = lambda b, i: (b, i, 0)
    cmap = lambda b, i: (0, 0)
    in_specs = [pl.BlockSpec((1, tm, Hd), row)]
    args = [u]
    if halo:
        in_specs += [
            pl.BlockSpec((1, grid_w, Hd), lambda b, i: (b, jnp.maximum(i * rpt - 1, 0), 0)),
            pl.BlockSpec((1, grid_w, Hd), lambda b, i: (b, jnp.minimum((i + 1) * rpt, n_rows - 1), 0)),
        ]
        args += [u, u]
    in_specs += [
        pl.BlockSpec((1, tm, Hd), lambda b, i: (b, i, 1)),
        pl.BlockSpec((9, Hd), cmap),
        pl.BlockSpec((1, Hd), cmap),
        pl.BlockSpec((Hd, D), cmap),
        pl.BlockSpec((1, tm, D), row),
        pl.BlockSpec((1, 1, D), gmap),
    ]
    args += [u, conv_w9, conv_b.reshape(1, Hd), wd, x, gate]
    if final_w is not None:
        in_specs.append(pl.BlockSpec((1, D), cmap))
        args.append(final_w.reshape(1, D))
    return pl.pallas_call(
        functools.partial(_ffn_down_kernel, grid_w=grid_w, halo=halo, final_norm=final_w is not None),
        grid=(Bx, Tx // tm),
        in_specs=in_specs,
        out_specs=pl.BlockSpec((1, tm, D), row),
        out_shape=jax.ShapeDtypeStruct((Bx, Tx, D), F32),
        scratch_shapes=[pltpu.VMEM((tm, Hd), BF16)],
        name="ffn_conv_down",
        compiler_params=_cparams(("arbitrary", "arbitrary")),
    )(*args)


def kernel(x, c, ctx, c_ctx, ada_w, ada_b, norm1_w, w_in, gla_lr_w, gla_lr_b, gla_norm_w, rnn_conv_w,
           rnn_conv_b, rnn_wa, rnn_ba, rnn_wx, rnn_bx, rnn_lambda, w_gla_o, w_rnn_o, w_out, norm2_w,
           ffn_up, ffn_conv_w, ffn_conv_b, ffn_down, final_norm_w):
    B, T, D = x.shape
    Tc = ctx.shape[1]
    depth = w_in.shape[0]

    cc = jnp.zeros((16, D), F32).at[:B].set(c).at[B].set(c_ctx)
    mod = _ada(cc, ada_w, ada_b)
    ctxf = ctx.reshape(1, B * Tc, D)

    for l in range(depth):
        last = l == depth - 1
        mx = mod[l, :B].reshape(B, 1, 6 * D)
        mc = mod[l, B:B + 1].reshape(1, 1, 6 * D)
        sh1, sc1, g1, sh2, sc2, g2 = [mx[:, :, k * D:(k + 1) * D] for k in range(6)]
        csh1, csc1, cg1, csh2, csc2, cg2 = [mc[:, :, k * D:(k + 1) * D] for k in range(6)]

        w_z = jnp.concatenate([w_in[l, :, :LR_START], w_in[l, :, LR_START + 2 * GLA_LOWRANK:]], axis=1).astype(BF16)
        w_lr = jnp.pad(w_in[l, :, LR_START:LR_START + 2 * GLA_LOWRANK],
                       ((0, 0), (0, LANES - 2 * GLA_LOWRANK))).astype(BF16)
        lrw = jnp.zeros((2, LANES, GLA_KW), F32)
        lrw = lrw.at[0, :GLA_LOWRANK].set(gla_lr_w[l, 0]).at[1, GLA_LOWRANK:2 * GLA_LOWRANK].set(gla_lr_w[l, 1])
        lrw = lrw.astype(BF16)
        wgate = jnp.concatenate([rnn_wa[l, 0], rnn_wx[l, 0], rnn_wa[l, 1], rnn_wx[l, 1]], axis=-1).astype(BF16)
        bgate = jnp.concatenate(
            [v.reshape(RNN_BLOCKS, 1, RNN_BLOCK) for v in (rnn_ba[l, 0], rnn_bx[l, 0], rnn_ba[l, 1], rnn_bx[l, 1])],
            axis=-1)
        wg_o, wr_o, wo = w_gla_o[l].astype(BF16), w_rnn_o[l].astype(BF16), w_out[l].astype(BF16)
        w_up, w_dn = ffn_up[l].astype(BF16), ffn_down[l].astype(BF16)
        conv9 = ffn_conv_w[l].reshape(9, FFN_HIDDEN)

        z, lr = _proj(x, norm1_w[l], sh1, sc1, w_z, w_lr, _IN_ACTS)
        zc, lrc = _proj(ctxf, norm1_w[l], csh1, csc1, w_z, w_lr, _IN_ACTS)
        zc = zc.reshape(B, Tc, Z_WIDTH)
        lrc = lrc.reshape(B, Tc, LANES)
        on, onc = _gla(z, lr, zc, lrc, lrw, gla_lr_b[l], gla_norm_w[l], ctx_out=not last)
        r, rc = _rglru(z, zc, rnn_conv_w[l], rnn_conv_b[l], wgate, bgate, rnn_lambda[l], ctx_out=not last)
        x = _mix(on, r, z, x, g1, wg_o, wr_o, wo)

        u = _proj(x, norm2_w[l], sh2, sc2, w_up, None, _UP_ACTS)
        x = _ffn_down(u, x, g2, conv9, ffn_conv_b[l], w_dn, GRID_W, final_norm_w if last else None)

        if not last:
            ctxf = _mix(onc.reshape(1, B * Tc, D), rc.reshape(1, B * Tc, D), zc.reshape(1, B * Tc, Z_WIDTH),
                        ctxf, cg1, wg_o, wr_o, wo)
            uc = _proj(ctxf, norm2_w[l], csh2, csc2, w_up, None, _UP_ACTS)
            ctxf = _ffn_down(uc.reshape(B, Tc, 2 * FFN_HIDDEN), ctxf.reshape(B, Tc, D), cg2, conv9,
                             ffn_conv_b[l], w_dn, Tc, None).reshape(1, B * Tc, D)
    return x
```

```python
import functools

import jax
import jax.numpy as jnp
from jax import lax
from jax.experimental import pallas as pl
from jax.experimental.pallas import tpu as pltpu

F32 = jnp.float32
BF16 = jnp.bfloat16

D_MODEL = 1024
GRID_W = 64
GLA_HEADS = 4
GLA_DK = 128
GLA_DV = 256
GLA_KW = GLA_HEADS * GLA_DK
GLA_VW = GLA_HEADS * GLA_DV
GLA_LOWRANK = 16
GLA_TAU = 16.0
GLA_CHUNK = 64
GLA_GROUP = 256
RNN_WIDTH = 1024
RNN_BLOCKS = 8
RNN_BLOCK = RNN_WIDTH // RNN_BLOCKS
RGLRU_C = 8.0
FFN_HIDDEN = 2816
NORM_EPS = 1e-6

LANES = 128
SCAN_SEGS = 8
VMEM_LIMIT = 56 * 1024 * 1024

Z_Q, Z_K, Z_V, Z_G, Z_XR, Z_YR, Z_GA, Z_GB = 0, 512, 1024, 2048, 3072, 4096, 5120, 6144
Z_WIDTH = 7168
LR_START = 2 * GLA_KW + 2 * GLA_VW
PROJ_TM = 1024
PROJ_TN = 512
MIX_TM = 512
FFN_TM = 512
FFN_CK = 128


def _cparams(sem):
    return pltpu.CompilerParams(dimension_semantics=sem, vmem_limit_bytes=VMEM_LIMIT)


def _sigmoid(x):
    return 0.5 + 0.5 * jnp.tanh(0.5 * x)


def _silu(x):
    return x * _sigmoid(x)


def _gelu_tanh(x):
    return 0.5 * x * (1.0 + jnp.tanh(0.7978845608028654 * (x + 0.044715 * (x * x * x))))


def _log_sigmoid(x):
    return jnp.minimum(x, 0.0) - jnp.log1p(jnp.exp(-jnp.abs(x)))


def _ada_kernel(c_ref, w_ref, b_ref, o_ref):
    s = _silu(c_ref[...])
    o_ref[0] = jnp.dot(s.astype(BF16), w_ref[0].astype(BF16), preferred_element_type=F32) + b_ref[0]


def _ada(cc, ada_w, ada_b):
    L, D, N = ada_w.shape
    tn = 1536
    return pl.pallas_call(
        _ada_kernel,
        grid=(L, N // tn),
        in_specs=[
            pl.BlockSpec((16, D), lambda l, n: (0, 0)),
            pl.BlockSpec((1, D, tn), lambda l, n: (l, 0, n)),
            pl.BlockSpec((1, 1, tn), lambda l, n: (l, 0, n)),
        ],
        out_specs=pl.BlockSpec((1, 16, tn), lambda l, n: (l, 0, n)),
        out_shape=jax.ShapeDtypeStruct((L, 16, N), F32),
        name="ada_mod",
        compiler_params=_cparams(("arbitrary", "arbitrary")),
    )(cc, ada_w, ada_b.reshape(L, 1, N))


def _proj_kernel(*refs, has_lr):
    if has_lr:
        x_ref, nw_ref, sh_ref, sc_ref, w_ref, wlr_ref, o_ref, olr_ref, xn_scr = refs
    else:
        x_ref, nw_ref, sh_ref, sc_ref, w_ref, o_ref, xn_scr = refs
    n = pl.program_id(2)

    @pl.when(n == 0)
    def _():
        x = x_ref[0]
        ms = jnp.mean(x * x, axis=-1, keepdims=True)
        y = x * lax.rsqrt(ms + NORM_EPS) * nw_ref[...]
        y = y * (1.0 + sc_ref[0]) + sh_ref[0]
        xb = y.astype(BF16)
        xn_scr[...] = xb
        if has_lr:
            olr_ref[0] = jnp.dot(xb, wlr_ref[...], preferred_element_type=F32).astype(olr_ref.dtype)

    o_ref[0] = jnp.dot(xn_scr[...], w_ref[...], preferred_element_type=F32).astype(o_ref.dtype)


def _proj(x, norm_w, shift, scale, w, wlr):
    Bx, Tx, D = x.shape
    N = w.shape[1]
    tm = min(PROJ_TM, Tx)
    tn = PROJ_TN
    per_batch = shift.shape[0] > 1
    mod_map = (lambda b, i, n: (b, 0, 0)) if per_batch else (lambda b, i, n: (0, 0, 0))
    has_lr = wlr is not None
    in_specs = [
        pl.BlockSpec((1, tm, D), lambda b, i, n: (b, i, 0)),
        pl.BlockSpec((1, D), lambda b, i, n: (0, 0)),
        pl.BlockSpec((1, 1, D), mod_map),
        pl.BlockSpec((1, 1, D), mod_map),
        pl.BlockSpec((D, tn), lambda b, i, n: (0, n)),
    ]
    args = [x, norm_w.reshape(1, D), shift, scale, w]
    out_specs = [pl.BlockSpec((1, tm, tn), lambda b, i, n: (b, i, n))]
    out_shape = [jax.ShapeDtypeStruct((Bx, Tx, N), BF16)]
    if has_lr:
        in_specs.append(pl.BlockSpec((D, LANES), lambda b, i, n: (0, 0)))
        args.append(wlr)
        out_specs.append(pl.BlockSpec((1, tm, LANES), lambda b, i, n: (b, i, 0)))
        out_shape.append(jax.ShapeDtypeStruct((Bx, Tx, LANES), BF16))
    res = pl.pallas_call(
        functools.partial(_proj_kernel, has_lr=has_lr),
        grid=(Bx, Tx // tm, N // tn),
        in_specs=in_specs,
        out_specs=out_specs,
        out_shape=out_shape,
        scratch_shapes=[pltpu.VMEM((tm, D), BF16)],
        name="norm_mod_proj",
        compiler_params=_cparams(("arbitrary", "arbitrary", "arbitrary")),
    )(*args)
    return res if has_lr else res[0]


_NT = (((1,), (1,)), ((), ()))
_TN = (((0,), (0,)), ((), ()))


def _gla_kernel(q_ref, k_ref, v_ref, g_ref, lr_ref, qc_ref, kc_ref, vc_ref, gc_ref, lrc_ref,
                lrw_ref, lrb_ref, nw_ref, o_ref, oc_ref,
                la_scr, oacc_scr, s_scr, kv_scr, sall_scr, qd_scr, dec_scr, *, ctx_out):
    C, G = GLA_CHUNK, GLA_GROUP
    CPG = G // C
    row = lax.broadcasted_iota(jnp.int32, (G, G), 0)
    col = lax.broadcasted_iota(jnp.int32, (G, G), 1)
    assert C & (C - 1) == 0
    same = (row ^ col) < C
    masks = (same & (col <= row), same & (col >= row))
    tris = tuple(jnp.where(m, 1.0, 0.0).astype(BF16) for m in masks)
    qscale = GLA_DK ** -0.5

    DK = GLA_DK
    FWD, BWD = slice(0, DK), slice(DK, 2 * DK)

    def run(q_r, k_r, v_r, g_r, lr_r, out_r, T, with_out):
        NG = T // G
        NC = T // C
        lrv = lr_r[0]
        for d in range(2):
            xx = jnp.dot(lrv, lrw_ref[d], preferred_element_type=F32) + lrb_ref[d:d + 1, :]
            la_scr[d, 0:T, :] = _log_sigmoid(xx) * (1.0 / GLA_TAU)

        def group(gi, carry):
            r0 = pl.multiple_of(gi * G, G)
            rows = pl.ds(r0, G)
            hls = []
            for d in range(2):
                la = la_scr[d, rows, :]
                hi = la.astype(BF16)
                hls += [hi, (la - hi.astype(F32)).astype(BF16)]
            kf = k_r[0, rows, :].astype(F32)
            vb = v_r[0, rows, :]
            if with_out:
                qf = q_r[0, rows, :].astype(F32) * qscale
            kds, qds, bls, comb = [], [], [], None
            for d in range(2):
                hl = jnp.concatenate(hls[2 * d:2 * d + 2], axis=1)
                cs = jnp.dot(tris[d], hl, preferred_element_type=F32)
                b = cs[:, :DK] + cs[:, DK:]
                ends = [(c + 1) * C - 1 if d == 0 else c * C for c in range(CPG)]
                bl = jnp.concatenate([jnp.broadcast_to(b[e:e + 1, :], (C, DK)) for e in ends], axis=0)
                bls.append(bl)
                kds.append((kf * jnp.exp(bl - b)).astype(BF16))
                if with_out:
                    qd = (qf * jnp.exp(b)).astype(BF16)
                    ki = (kf * jnp.exp(-b)).astype(BF16)
                    sc = jnp.where(masks[d], lax.dot_general(qd, ki, _NT, preferred_element_type=F32), 0.0)
                    comb = sc if comb is None else comb + sc
                    qds.append(qd)
            if with_out:
                oacc_scr[rows, :] = jnp.dot(comb.astype(BF16), vb, preferred_element_type=F32)
                qd_scr[rows, :] = jnp.concatenate(qds, axis=1)
            kd = jnp.concatenate(kds, axis=1)
            dec = jnp.exp(jnp.concatenate(bls, axis=1))
            for c in range(CPG):
                cr = slice(c * C, (c + 1) * C)
                n = gi * CPG + c
                kv_scr[n] = lax.dot_general(vb[cr, :], kd[cr, :], _TN, preferred_element_type=F32)
                dec_scr[n] = dec[c * C:c * C + 8, :]
            return carry

        lax.fori_loop(0, NG, group, 0, unroll=2)

        def scan(n, st):
            sf, sb = st
            nb = NC - 1 - n
            sall_scr[n, :, FWD] = sf.astype(BF16)
            sall_scr[nb, :, BWD] = sb.astype(BF16)
            sf = sf * dec_scr[n, 0:1, FWD] + kv_scr[n, :, FWD]
            sb = sb * dec_scr[nb, 0:1, BWD] + kv_scr[nb, :, BWD]
            return sf, sb

        sf, sb = lax.fori_loop(0, NC, scan, (s_scr[:, FWD], s_scr[:, BWD]), unroll=2)
        s_scr[:, FWD] = sf
        s_scr[:, BWD] = sb

        if with_out:
            def inter(gi, carry):
                for c in range(CPG):
                    n = gi * CPG + c
                    rows = pl.ds(pl.multiple_of(n * C, C), C)
                    oe = lax.dot_general(qd_scr[rows, :], sall_scr[n], _NT, preferred_element_type=F32)
                    oacc_scr[rows, :] = oacc_scr[rows, :] + oe
                return carry

            lax.fori_loop(0, NG, inter, 0, unroll=2)

            o = oacc_scr[0:T, :]
            inv = lax.rsqrt(jnp.mean(o * o, axis=-1, keepdims=True) + NORM_EPS)
            out_r[0] = (o * inv * nw_ref[...] * _silu(g_r[0].astype(F32))).astype(out_r.dtype)

    Tc = qc_ref.shape[1]
    T = q_ref.shape[1]
    s_scr[...] = jnp.zeros_like(s_scr)
    run(qc_ref, kc_ref, vc_ref, gc_ref, lrc_ref, oc_ref, Tc, ctx_out)
    if not ctx_out:
        oc_ref[...] = jnp.zeros_like(oc_ref)
    run(q_ref, k_ref, v_ref, g_ref, lr_ref, o_ref, T, True)


def _gla(z, lr, zc, lrc, lrw, lrb, norm_w, ctx_out):
    B, T, _ = z.shape
    Tc = zc.shape[1]
    H = GLA_HEADS

    def specs(Tx):
        return [
            pl.BlockSpec((1, Tx, GLA_DK), lambda b, h: (b, 0, Z_Q // GLA_DK + h)),
            pl.BlockSpec((1, Tx, GLA_DK), lambda b, h: (b, 0, Z_K // GLA_DK + h)),
            pl.BlockSpec((1, Tx, GLA_DV), lambda b, h: (b, 0, Z_V // GLA_DV + h)),
            pl.BlockSpec((1, Tx, GLA_DV), lambda b, h: (b, 0, Z_G // GLA_DV + h)),
            pl.BlockSpec((1, Tx, LANES), lambda b, h: (b, 0, 0)),
        ]

    in_specs = specs(T) + specs(Tc) + [
        pl.BlockSpec((2, LANES, GLA_DK), lambda b, h: (0, 0, h)),
        pl.BlockSpec((2, GLA_DK), lambda b, h: (0, h)),
        pl.BlockSpec((1, GLA_DV), lambda b, h: (0, h)),
    ]
    return pl.pallas_call(
        functools.partial(_gla_kernel, ctx_out=ctx_out),
        grid=(B, H),
        in_specs=in_specs,
        out_specs=[
            pl.BlockSpec((1, T, GLA_DV), lambda b, h: (b, 0, h)),
            pl.BlockSpec((1, Tc, GLA_DV), lambda b, h: (b, 0, h)),
        ],
        out_shape=[
            jax.ShapeDtypeStruct((B, T, GLA_VW), BF16),
            jax.ShapeDtypeStruct((B, Tc, GLA_VW), BF16),
        ],
        scratch_shapes=[
            pltpu.VMEM((2, T, GLA_DK), F32),
            pltpu.VMEM((T, GLA_DV), F32),
            pltpu.VMEM((GLA_DV, 2 * GLA_DK), F32),
            pltpu.VMEM((T // GLA_CHUNK, GLA_DV, 2 * GLA_DK), F32),
            pltpu.VMEM((T // GLA_CHUNK, GLA_DV, 2 * GLA_DK), BF16),
            pltpu.VMEM((T, 2 * GLA_DK), BF16),
            pltpu.VMEM((T // GLA_CHUNK, 8, 2 * GLA_DK), F32),
        ],
        name="gla_bidir",
        compiler_params=_cparams(("arbitrary", "arbitrary")),
    )(z, z, z, z, lr, zc, zc, zc, zc, lrc, lrw, lrb, norm_w.reshape(1, GLA_VW))


def _rglru_kernel(xr_ref, xrc_ref, cw_ref, cb_ref, wg_ref, bg_ref, lam_ref, r_ref, rc_ref,
                  af_scr, uf_scr, ab_scr, ub_scr, pf_scr, hf_scr, pb_scr, hb_scr, *, ctx_out):
    W = RNN_BLOCK
    a_scrs = (af_scr, ab_scr)
    u_scrs = (uf_scr, ub_scr)
    log_sig_lam = _log_sigmoid(lam_ref[...])

    def stream(x_r, out_r, h0, with_out):
        T = x_r.shape[1]
        L = T // SCAN_SEGS
        P = L + 8
        x = x_r[0].astype(F32)
        sub = lax.broadcasted_iota(jnp.int32, (8, W), 0)

        def shifted(k):
            y = pltpu.roll(x, (T - k) % T, 0)
            if k < 0:
                return jnp.concatenate([jnp.where(sub >= -k, y[0:8, :], 0.0), y[8:, :]], axis=0)
            return jnp.concatenate([y[:T - 8, :], jnp.where(sub < 8 - k, y[T - 8:, :], 0.0)], axis=0)

        xm1, xp1, xp2 = shifted(-1), shifted(1), shifted(2)
        xc = (cw_ref[0:1, :] * xm1 + cw_ref[1:2, :] * x + cw_ref[2:3, :] * xp1
              + cw_ref[3:4, :] * xp2 + cb_ref[...])
        gates = jnp.dot(xc.astype(BF16), wg_ref[0], preferred_element_type=F32) + bg_ref[0]
        for d in range(2):
            rg = _sigmoid(gates[:, (2 * d) * W:(2 * d + 1) * W])
            ig = _sigmoid(gates[:, (2 * d + 1) * W:(2 * d + 2) * W])
            log_a = RGLRU_C * rg * log_sig_lam[d:d + 1, :]
            a = jnp.exp(log_a)
            th = jnp.tanh(log_a)
            u = xc * ig * jnp.sqrt(-2.0 * th / (1.0 - th))
            for j in range(SCAN_SEGS):
                a_scrs[d][j * P:j * P + L, :] = a[j * L:(j + 1) * L, :]
                u_scrs[d][j * P:j * P + L, :] = u[j * L:(j + 1) * L, :]

        def body(i, carry):
            hf, pf, hb, pb = carry
            rf = pl.ds(i, SCAN_SEGS, stride=P)
            a = af_scr[rf, :]
            hf = a * hf + uf_scr[rf, :]
            pf = pf * a
            hf_scr[rf, :] = hf
            pf_scr[rf, :] = pf
            rb = pl.ds(L - 1 - i, SCAN_SEGS, stride=P)
            a = ab_scr[rb, :]
            hb = a * hb + ub_scr[rb, :]
            pb = pb * a
            hb_scr[rb, :] = hb
            pb_scr[rb, :] = pb
            return hf, pf, hb, pb

        zero = jnp.zeros((SCAN_SEGS, W), F32)
        one = jnp.ones((SCAN_SEGS, W), F32)
        hf, pf, hb, pb = lax.fori_loop(0, L, body, (zero, one, zero, one), unroll=4)

        c = h0[0]
        cf = []
        for j in range(SCAN_SEGS):
            cf.append(c)
            c = hf[j:j + 1, :] + pf[j:j + 1, :] * c
        fin_f = c
        c = h0[1]
        cb = [None] * SCAN_SEGS
        for j in reversed(range(SCAN_SEGS)):
            cb[j] = c
            c = hb[j:j + 1, :] + pb[j:j + 1, :] * c
        fin_b = c
        if with_out:
            for j in range(SCAN_SEGS):
                seg = slice(j * P, j * P + L)
                h = (hf_scr[seg, :] + pf_scr[seg, :] * cf[j]) + (hb_scr[seg, :] + pb_scr[seg, :] * cb[j])
                out_r[0, j * L:(j + 1) * L, :] = h.astype(out_r.dtype)
        return fin_f, fin_b

    zrow = jnp.zeros((1, W), F32)
    states = stream(xrc_ref, rc_ref, (zrow, zrow), ctx_out)
    if not ctx_out:
        rc_ref[...] = jnp.zeros_like(rc_ref)
    stream(xr_ref, r_ref, states, True)


def _rglru(z, zc, conv_w, conv_b, wg, bg, lam, ctx_out):
    B, T, _ = z.shape
    Tc = zc.shape[1]
    G, W = RNN_BLOCKS, RNN_BLOCK
    xoff = Z_XR // W
    seg_rows = SCAN_SEGS * (T // SCAN_SEGS + 8)
    return pl.pallas_call(
        functools.partial(_rglru_kernel, ctx_out=ctx_out),
        grid=(B, G),
        in_specs=[
            pl.BlockSpec((1, T, W), lambda b, g: (b, 0, xoff + g)),
            pl.BlockSpec((1, Tc, W), lambda b, g: (b, 0, xoff + g)),
            pl.BlockSpec((4, W), lambda b, g: (0, g)),
            pl.BlockSpec((1, W), lambda b, g: (0, g)),
            pl.BlockSpec((1, W, 4 * W), lambda b, g: (g, 0, 0)),
            pl.BlockSpec((1, 1, 4 * W), lambda b, g: (g, 0, 0)),
            pl.BlockSpec((2, W), lambda b, g: (0, g)),
        ],
        out_specs=[
            pl.BlockSpec((1, T, W), lambda b, g: (b, 0, g)),
            pl.BlockSpec((1, Tc, W), lambda b, g: (b, 0, g)),
        ],
        out_shape=[
            jax.ShapeDtypeStruct((B, T, RNN_WIDTH), BF16),
            jax.ShapeDtypeStruct((B, Tc, RNN_WIDTH), BF16),
        ],
        scratch_shapes=[pltpu.VMEM((seg_rows, W), F32) for _ in range(8)],
        name="rglru_bidir",
        compiler_params=_cparams(("arbitrary", "arbitrary")),
    )(z, zc, conv_w, conv_b.reshape(1, RNN_WIDTH), wg, bg, lam)


def _mix_kernel(on_ref, h_ref, yr_ref, ga_ref, gb_ref, x_ref, g1_ref, wg_ref, wr_ref, wo_ref, o_ref):
    r = (h_ref[0].astype(F32) * _gelu_tanh(yr_ref[0].astype(F32))).astype(BF16)
    ya = jnp.dot(on_ref[0], wg_ref[...], preferred_element_type=F32) * _sigmoid(ga_ref[0].astype(F32))
    yb = jnp.dot(r, wr_ref[...], preferred_element_type=F32) * _sigmoid(gb_ref[0].astype(F32))
    y = jnp.dot((ya + yb).astype(BF16), wo_ref[...], preferred_element_type=F32)
    o_ref[0] = x_ref[0] + g1_ref[0] * y


def _mix(on, h, z, x, gate, wg, wr, wo):
    Bx, Tx, D = x.shape
    tm = min(MIX_TM, Tx)
    per_batch = gate.shape[0] > 1
    gmap = (lambda b, i: (b, 0, 0)) if per_batch else (lambda b, i: (0, 0, 0))
    row = lambda b, i: (b, i, 0)
    wmap = lambda b, i: (0, 0)
    return pl.pallas_call(
        _mix_kernel,
        grid=(Bx, Tx // tm),
        in_specs=[
            pl.BlockSpec((1, tm, D), row),
            pl.BlockSpec((1, tm, D), row),
            pl.BlockSpec((1, tm, D), lambda b, i: (b, i, Z_YR // D_MODEL)),
            pl.BlockSpec((1, tm, D), lambda b, i: (b, i, Z_GA // D_MODEL)),
            pl.BlockSpec((1, tm, D), lambda b, i: (b, i, Z_GB // D_MODEL)),
            pl.BlockSpec((1, tm, D), row),
            pl.BlockSpec((1, 1, D), gmap),
            pl.BlockSpec((D, D), wmap),
            pl.BlockSpec((D, D), wmap),
            pl.BlockSpec((D, D), wmap),
        ],
        out_specs=pl.BlockSpec((1, tm, D), row),
        out_shape=jax.ShapeDtypeStruct((Bx, Tx, D), F32),
        name="mix_out",
        compiler_params=_cparams(("arbitrary", "arbitrary")),
    )(on, h, z, z, z, x, gate, wg, wr, wo)


def _ffn_down_kernel(*refs, grid_w, halo, final_norm):
    if halo:
        a_ref, up_ref, dn_ref, gv_ref = refs[:4]
        rest = refs[4:]
    else:
        a_ref, gv_ref = refs[:2]
        rest = refs[2:]
    if final_norm:
        cw_ref, cb_ref, wd_ref, x_ref, g2_ref, fw_ref, o_ref, h_scr = rest
    else:
        cw_ref, cb_ref, wd_ref, x_ref, g2_ref, o_ref, h_scr = rest
    TM = a_ref.shape[1]
    CK = FFN_CK
    i = pl.program_id(1)
    n_i = pl.num_programs(1)
    ext_rows = TM + 2 * grid_w if halo else TM
    assert grid_w & (grid_w - 1) == 0
    colidx = lax.broadcasted_iota(jnp.int32, (ext_rows, CK), 0) & (grid_w - 1)
    first_col = colidx == 0
    last_col = colidx == grid_w - 1
    if halo:
        up_scale = jnp.where(i > 0, 1.0, 0.0).astype(F32)
        dn_scale = jnp.where(i < n_i - 1, 1.0, 0.0).astype(F32)

    def conv_chunk(c):
        c0 = pl.multiple_of(c * CK, CK)
        cs = pl.ds(c0, CK)
        am = a_ref[0, :, cs].astype(F32)
        if halo:
            ext = jnp.concatenate(
                [up_ref[0, :, cs].astype(F32) * up_scale, am, dn_ref[0, :, cs].astype(F32) * dn_scale], axis=0)
        else:
            ext = am
        left = jnp.where(first_col, 0.0, pltpu.roll(ext, 1, 0))
        right = jnp.where(last_col, 0.0, pltpu.roll(ext, ext_rows - 1, 0))
        acc = cb_ref[:, cs]
        for kh in ((0, 1, 2) if halo else (1,)):
            lo = kh * grid_w if halo else 0
            acc = acc + (cw_ref[3 * kh:3 * kh + 1, cs] * left[lo:lo + TM, :]
                         + cw_ref[3 * kh + 1:3 * kh + 2, cs] * ext[lo:lo + TM, :]
                         + cw_ref[3 * kh + 2:3 * kh + 3, cs] * right[lo:lo + TM, :])
        h_scr[:, cs] = (_gelu_tanh(acc) * gv_ref[0, :, cs].astype(F32)).astype(BF16)

    def chunk(c, carry):
        conv_chunk(c)
        return carry

    lax.fori_loop(0, FFN_HIDDEN // CK, chunk, 0)
    y = jnp.dot(h_scr[...], wd_ref[...], preferred_element_type=F32)
    xo = x_ref[0] + g2_ref[0] * y
    if final_norm:
        ms = jnp.mean(xo * xo, axis=-1, keepdims=True)
        xo = xo * lax.rsqrt(ms + NORM_EPS) * fw_ref[...]
    o_ref[0] = xo


def _ffn_down(u, x, gate, conv_w9, conv_b, wd, grid_w, final_w):
    Bx, Tx, D = x.shape
    Hd = FFN_HIDDEN
    halo = grid_w < Tx
    tm = min(FFN_TM, Tx)
    rpt = tm // grid_w if halo else 1
    n_rows = Tx // grid_w if halo else 1
    per_batch = gate.shape[0] > 1
    gmap = (lambda b, i: (b, 0, 0)) if per_batch else (lambda b, i: (0, 0, 0))
    row = lambda b, i: (b, i, 0)
    cmap = lambda b, i: (0, 0)
    in_specs = [pl.BlockSpec((1, tm, Hd), row)]
    args = [u]
    if halo:
        in_specs += [
            pl.BlockSpec((1, grid_w, Hd), lambda b, i: (b, jnp.maximum(i * rpt - 1, 0), 0)),
            pl.BlockSpec((1, grid_w, Hd), lambda b, i: (b, jnp.minimum((i + 1) * rpt, n_rows - 1), 0)),
        ]
        args += [u, u]
    in_specs += [
        pl.BlockSpec((1, tm, Hd), lambda b, i: (b, i, 1)),
        pl.BlockSpec((9, Hd), cmap),
        pl.BlockSpec((1, Hd), cmap),
        pl.BlockSpec((Hd, D), cmap),
        pl.BlockSpec((1, tm, D), row),
        pl.BlockSpec((1, 1, D), gmap),
    ]
    args += [u, conv_w9, conv_b.reshape(1, Hd), wd, x, gate]
    if final_w is not None:
        in_specs.append(pl.BlockSpec((1, D), cmap))
        args.append(final_w.reshape(1, D))
    return pl.pallas_call(
        functools.partial(_ffn_down_kernel, grid_w=grid_w, halo=halo, final_norm=final_w is not None),
        grid=(Bx, Tx // tm),
        in_specs=in_specs,
        out_specs=pl.BlockSpec((1, tm, D), row),
        out_shape=jax.ShapeDtypeStruct((Bx, Tx, D), F32),
        scratch_shapes=[pltpu.VMEM((tm, Hd), BF16)],
        name="ffn_conv_down",
        compiler_params=_cparams(("arbitrary", "arbitrary")),
    )(*args)


def kernel(x, c, ctx, c_ctx, ada_w, ada_b, norm1_w, w_in, gla_lr_w, gla_lr_b, gla_norm_w, rnn_conv_w,
           rnn_conv_b, rnn_wa, rnn_ba, rnn_wx, rnn_bx, rnn_lambda, w_gla_o, w_rnn_o, w_out, norm2_w,
           ffn_up, ffn_conv_w, ffn_conv_b, ffn_down, final_norm_w):
    B, T, D = x.shape
    Tc = ctx.shape[1]
    depth = w_in.shape[0]

    cc = jnp.zeros((16, D), F32).at[:B].set(c).at[B].set(c_ctx)
    mod = _ada(cc, ada_w, ada_b)
    ctxf = ctx.reshape(1, B * Tc, D)

    for l in range(depth):
        last = l == depth - 1
        mx = mod[l, :B].reshape(B, 1, 6 * D)
        mc = mod[l, B:B + 1].reshape(1, 1, 6 * D)
        sh1, sc1, g1, sh2, sc2, g2 = [mx[:, :, k * D:(k + 1) * D] for k in range(6)]
        csh1, csc1, cg1, csh2, csc2, cg2 = [mc[:, :, k * D:(k + 1) * D] for k in range(6)]

        w_z = jnp.concatenate([w_in[l, :, :LR_START], w_in[l, :, LR_START + 2 * GLA_LOWRANK:]], axis=1).astype(BF16)
        w_lr = jnp.pad(w_in[l, :, LR_START:LR_START + 2 * GLA_LOWRANK],
                       ((0, 0), (0, LANES - 2 * GLA_LOWRANK))).astype(BF16)
        lrw = jnp.zeros((2, LANES, GLA_KW), F32)
        lrw = lrw.at[0, :GLA_LOWRANK].set(gla_lr_w[l, 0]).at[1, GLA_LOWRANK:2 * GLA_LOWRANK].set(gla_lr_w[l, 1])
        lrw = lrw.astype(BF16)
        wgate = jnp.concatenate([rnn_wa[l, 0], rnn_wx[l, 0], rnn_wa[l, 1], rnn_wx[l, 1]], axis=-1).astype(BF16)
        bgate = jnp.concatenate(
            [v.reshape(RNN_BLOCKS, 1, RNN_BLOCK) for v in (rnn_ba[l, 0], rnn_bx[l, 0], rnn_ba[l, 1], rnn_bx[l, 1])],
            axis=-1)
        wg_o, wr_o, wo = w_gla_o[l].astype(BF16), w_rnn_o[l].astype(BF16), w_out[l].astype(BF16)
        w_up, w_dn = ffn_up[l].astype(BF16), ffn_down[l].astype(BF16)
        conv9 = ffn_conv_w[l].reshape(9, FFN_HIDDEN)

        z, lr = _proj(x, norm1_w[l], sh1, sc1, w_z, w_lr)
        zc, lrc = _proj(ctxf, norm1_w[l], csh1, csc1, w_z, w_lr)
        zc = zc.reshape(B, Tc, Z_WIDTH)
        lrc = lrc.reshape(B, Tc, LANES)
        on, onc = _gla(z, lr, zc, lrc, lrw, gla_lr_b[l], gla_norm_w[l], ctx_out=not last)
        r, rc = _rglru(z, zc, rnn_conv_w[l], rnn_conv_b[l], wgate, bgate, rnn_lambda[l], ctx_out=not last)
        x = _mix(on, r, z, x, g1, wg_o, wr_o, wo)

        u = _proj(x, norm2_w[l], sh2, sc2, w_up, None)
        x = _ffn_down(u, x, g2, conv9, ffn_conv_b[l], w_dn, GRID_W, final_norm_w if last else None)

        if not last:
            ctxf = _mix(onc.reshape(1, B * Tc, D), rc.reshape(1, B * Tc, D), zc.reshape(1, B * Tc, Z_WIDTH),
                        ctxf, cg1, wg_o, wr_o, wo)
            uc = _proj(ctxf, norm2_w[l], csh2, csc2, w_up, None)
            ctxf = _ffn_down(uc.reshape(B, Tc, 2 * FFN_HIDDEN), ctxf.reshape(B, Tc, D), cg2, conv9,
                             ffn_conv_b[l], w_dn, Tc, None).reshape(1, B * Tc, D)
    return x
```

```python
import functools

import jax
import jax.numpy as jnp
from jax import lax
from jax.experimental import pallas as pl
from jax.experimental.pallas import tpu as pltpu

F32 = jnp.float32
BF16 = jnp.bfloat16

D_MODEL = 1024
GRID_W = 64
GLA_HEADS = 4
GLA_DK = 128
GLA_DV = 256
GLA_KW = GLA_HEADS * GLA_DK
GLA_VW = GLA_HEADS * GLA_DV
GLA_LOWRANK = 16
GLA_TAU = 16.0
GLA_CHUNK = 64
GLA_GROUP = 256
RNN_WIDTH = 1024
RNN_BLOCKS = 8
RNN_BLOCK = RNN_WIDTH // RNN_BLOCKS
RGLRU_C = 8.0
FFN_HIDDEN = 2816
NORM_EPS = 1e-6

LANES = 128
SCAN_SEGS = 8
VMEM_LIMIT = 56 * 1024 * 1024

Z_Q, Z_K, Z_V, Z_G, Z_XR, Z_YR, Z_GA, Z_GB = 0, 512, 1024, 2048, 3072, 4096, 5120, 6144
Z_WIDTH = 7168
LR_START = 2 * GLA_KW + 2 * GLA_VW
PROJ_TM = 1024
PROJ_TN_IN = 1792
PROJ_TN_UP = 2816
MIX_TM = 512
FFN_TM = 512
FFN_CK = 128


def _cparams(sem):
    return pltpu.CompilerParams(dimension_semantics=sem, vmem_limit_bytes=VMEM_LIMIT)


def _sigmoid(x):
    return 0.5 + 0.5 * jnp.tanh(0.5 * x)


def _silu(x):
    return x * _sigmoid(x)


_GELU_C = 0.7978845608028654


def _gelu_tanh(x):
    return 0.5 * x * (1.0 + jnp.tanh(_GELU_C * (x + 0.044715 * (x * x * x))))


def _log_sigmoid(x):
    return jnp.minimum(x, 0.0) - jnp.log1p(jnp.exp(-jnp.abs(x)))


def _ada_kernel(c_ref, w_ref, b_ref, o_ref):
    s = _silu(c_ref[...])
    o_ref[0] = jnp.dot(s.astype(BF16), w_ref[0].astype(BF16), preferred_element_type=F32) + b_ref[0]


def _ada(cc, ada_w, ada_b):
    L, D, N = ada_w.shape
    tn = 1536
    return pl.pallas_call(
        _ada_kernel,
        grid=(L, N // tn),
        in_specs=[
            pl.BlockSpec((16, D), lambda l, n: (0, 0)),
            pl.BlockSpec((1, D, tn), lambda l, n: (l, 0, n)),
            pl.BlockSpec((1, 1, tn), lambda l, n: (l, 0, n)),
        ],
        out_specs=pl.BlockSpec((1, 16, tn), lambda l, n: (l, 0, n)),
        out_shape=jax.ShapeDtypeStruct((L, 16, N), F32),
        name="ada_mod",
        compiler_params=_cparams(("arbitrary", "arbitrary")),
    )(cc, ada_w, ada_b.reshape(L, 1, N))


def _proj_kernel(*refs, has_lr):
    if has_lr:
        x_ref, nw_ref, sh_ref, sc_ref, w_ref, wlr_ref, o_ref, olr_ref, xn_scr = refs
    else:
        x_ref, nw_ref, sh_ref, sc_ref, w_ref, o_ref, xn_scr = refs
    n = pl.program_id(2)

    @pl.when(n == 0)
    def _():
        x = x_ref[0]
        ms = jnp.mean(x * x, axis=-1, keepdims=True)
        y = x * lax.rsqrt(ms + NORM_EPS) * nw_ref[...]
        y = y * (1.0 + sc_ref[0]) + sh_ref[0]
        xb = y.astype(BF16)
        xn_scr[...] = xb
        if has_lr:
            olr_ref[0] = jnp.dot(xb, wlr_ref[...], preferred_element_type=F32).astype(olr_ref.dtype)

    o_ref[0] = jnp.dot(xn_scr[...], w_ref[...], preferred_element_type=F32).astype(o_ref.dtype)


def _proj(x, norm_w, shift, scale, w, wlr, l, tn):
    Bx, Tx, D = x.shape
    N = w.shape[2]
    tm = min(PROJ_TM, Tx)
    per_batch = shift.shape[0] > 1
    mod_map = (lambda b, i, n: (b, 0, 0)) if per_batch else (lambda b, i, n: (0, 0, 0))
    has_lr = wlr is not None
    in_specs = [
        pl.BlockSpec((1, tm, D), lambda b, i, n: (b, i, 0)),
        pl.BlockSpec((1, D), lambda b, i, n: (0, 0)),
        pl.BlockSpec((1, 1, D), mod_map),
        pl.BlockSpec((1, 1, D), mod_map),
        pl.BlockSpec((None, D, tn), lambda b, i, n: (l, 0, n)),
    ]
    args = [x, norm_w.reshape(1, D), shift, scale, w]
    out_specs = [pl.BlockSpec((1, tm, tn), lambda b, i, n: (b, i, n))]
    out_shape = [jax.ShapeDtypeStruct((Bx, Tx, N), BF16)]
    if has_lr:
        in_specs.append(pl.BlockSpec((None, D, LANES), lambda b, i, n: (l, 0, 0)))
        args.append(wlr)
        out_specs.append(pl.BlockSpec((1, tm, LANES), lambda b, i, n: (b, i, 0)))
        out_shape.append(jax.ShapeDtypeStruct((Bx, Tx, LANES), BF16))
    res = pl.pallas_call(
        functools.partial(_proj_kernel, has_lr=has_lr),
        grid=(Bx, Tx // tm, N // tn),
        in_specs=in_specs,
        out_specs=out_specs,
        out_shape=out_shape,
        scratch_shapes=[pltpu.VMEM((tm, D), BF16)],
        name="norm_mod_proj",
        compiler_params=_cparams(("arbitrary", "arbitrary", "arbitrary")),
    )(*args)
    return res if has_lr else res[0]


_NT = (((1,), (1,)), ((), ()))
_TN = (((0,), (0,)), ((), ()))


def _gla_kernel(q_ref, k_ref, v_ref, g_ref, lr_ref, qc_ref, kc_ref, vc_ref, gc_ref, lrc_ref,
                lrw_ref, lrb_ref, nw_ref, o_ref, oc_ref,
                la_scr, oacc_scr, s_scr, kv_scr, sall_scr, qd_scr, dec_scr, *, ctx_out):
    C, G = GLA_CHUNK, GLA_GROUP
    CPG = G // C
    row = lax.broadcasted_iota(jnp.int32, (G, G), 0)
    col = lax.broadcasted_iota(jnp.int32, (G, G), 1)
    assert C & (C - 1) == 0
    same = (row ^ col) < C
    masks = (same & (col <= row), same & (col >= row))
    tris = tuple(jnp.where(m, 1.0, 0.0).astype(BF16) for m in masks)
    qscale = GLA_DK ** -0.5

    DK = GLA_DK
    FWD, BWD = slice(0, DK), slice(DK, 2 * DK)

    def run(q_r, k_r, v_r, g_r, lr_r, out_r, T, with_out):
        NG = T // G
        NC = T // C
        lrv = lr_r[0]
        for d in range(2):
            xx = jnp.dot(lrv, lrw_ref[d], preferred_element_type=F32) + lrb_ref[d:d + 1, :]
            la_scr[d, 0:T, :] = _log_sigmoid(xx) * (1.0 / GLA_TAU)

        def group(gi, carry):
            r0 = pl.multiple_of(gi * G, G)
            rows = pl.ds(r0, G)
            hls = []
            for d in range(2):
                la = la_scr[d, rows, :]
                hi = la.astype(BF16)
                hls += [hi, (la - hi.astype(F32)).astype(BF16)]
            kf = k_r[0, rows, :].astype(F32)
            vb = v_r[0, rows, :]
            if with_out:
                qf = q_r[0, rows, :].astype(F32) * qscale
            kds, qds, bls, comb = [], [], [], None
            for d in range(2):
                hl = jnp.concatenate(hls[2 * d:2 * d + 2], axis=1)
                cs = jnp.dot(tris[d], hl, preferred_element_type=F32)
                b = cs[:, :DK] + cs[:, DK:]
                ends = [(c + 1) * C - 1 if d == 0 else c * C for c in range(CPG)]
                bl = jnp.concatenate([jnp.broadcast_to(b[e:e + 1, :], (C, DK)) for e in ends], axis=0)
                bls.append(bl)
                kds.append((kf * jnp.exp(bl - b)).astype(BF16))
                if with_out:
                    qd = (qf * jnp.exp(b)).astype(BF16)
                    ki = (kf * jnp.exp(-b)).astype(BF16)
                    sc = jnp.where(masks[d], lax.dot_general(qd, ki, _NT, preferred_element_type=F32), 0.0)
                    comb = sc if comb is None else comb + sc
                    qds.append(qd)
            if with_out:
                oacc_scr[rows, :] = jnp.dot(comb.astype(BF16), vb, preferred_element_type=F32)
                qd_scr[rows, :] = jnp.concatenate(qds, axis=1)
            kd = jnp.concatenate(kds, axis=1)
            dec = jnp.exp(jnp.concatenate(bls, axis=1))
            for c in range(CPG):
                cr = slice(c * C, (c + 1) * C)
                n = gi * CPG + c
                kv_scr[n] = lax.dot_general(vb[cr, :], kd[cr, :], _TN, preferred_element_type=F32)
                dec_scr[n] = dec[c * C:c * C + 8, :]
            return carry

        lax.fori_loop(0, NG, group, 0, unroll=4)

        def scan(n, st):
            sf, sb = st
            nb = NC - 1 - n
            sall_scr[n, :, FWD] = sf.astype(BF16)
            sall_scr[nb, :, BWD] = sb.astype(BF16)
            sf = sf * dec_scr[n, 0:1, FWD] + kv_scr[n, :, FWD]
            sb = sb * dec_scr[nb, 0:1, BWD] + kv_scr[nb, :, BWD]
            return sf, sb

        sf, sb = lax.fori_loop(0, NC, scan, (s_scr[:, FWD], s_scr[:, BWD]), unroll=2)
        s_scr[:, FWD] = sf
        s_scr[:, BWD] = sb

        if with_out:
            def inter(gi, carry):
                for c in range(CPG):
                    n = gi * CPG + c
                    rows = pl.ds(pl.multiple_of(n * C, C), C)
                    oe = lax.dot_general(qd_scr[rows, :], sall_scr[n], _NT, preferred_element_type=F32)
                    oacc_scr[rows, :] = oacc_scr[rows, :] + oe
                return carry

            lax.fori_loop(0, NG, inter, 0, unroll=2)

            o = oacc_scr[0:T, :]
            inv = lax.rsqrt(jnp.mean(o * o, axis=-1, keepdims=True) + NORM_EPS)
            out_r[0] = (o * inv * nw_ref[...] * _silu(g_r[0].astype(F32))).astype(out_r.dtype)

    Tc = qc_ref.shape[1]
    T = q_ref.shape[1]
    s_scr[...] = jnp.zeros_like(s_scr)
    run(qc_ref, kc_ref, vc_ref, gc_ref, lrc_ref, oc_ref, Tc, ctx_out)
    if not ctx_out:
        oc_ref[...] = jnp.zeros_like(oc_ref)
    run(q_ref, k_ref, v_ref, g_ref, lr_ref, o_ref, T, True)


def _gla(z, lr, zc, lrc, lrw, lrb, norm_w, ctx_out):
    B, T, _ = z.shape
    Tc = zc.shape[1]
    H = GLA_HEADS

    def specs(Tx):
        return [
            pl.BlockSpec((1, Tx, GLA_DK), lambda b, h: (b, 0, Z_Q // GLA_DK + h)),
            pl.BlockSpec((1, Tx, GLA_DK), lambda b, h: (b, 0, Z_K // GLA_DK + h)),
            pl.BlockSpec((1, Tx, GLA_DV), lambda b, h: (b, 0, Z_V // GLA_DV + h)),
            pl.BlockSpec((1, Tx, GLA_DV), lambda b, h: (b, 0, Z_G // GLA_DV + h)),
            pl.BlockSpec((1, Tx, LANES), lambda b, h: (b, 0, 0)),
        ]

    in_specs = specs(T) + specs(Tc) + [
        pl.BlockSpec((2, LANES, GLA_DK), lambda b, h: (0, 0, h)),
        pl.BlockSpec((2, GLA_DK), lambda b, h: (0, h)),
        pl.BlockSpec((1, GLA_DV), lambda b, h: (0, h)),
    ]
    return pl.pallas_call(
        functools.partial(_gla_kernel, ctx_out=ctx_out),
        grid=(B, H),
        in_specs=in_specs,
        out_specs=[
            pl.BlockSpec((1, T, GLA_DV), lambda b, h: (b, 0, h)),
            pl.BlockSpec((1, Tc, GLA_DV), lambda b, h: (b, 0, h)),
        ],
        out_shape=[
            jax.ShapeDtypeStruct((B, T, GLA_VW), BF16),
            jax.ShapeDtypeStruct((B, Tc, GLA_VW), BF16),
        ],
        scratch_shapes=[
            pltpu.VMEM((2, T, GLA_DK), F32),
            pltpu.VMEM((T, GLA_DV), F32),
            pltpu.VMEM((GLA_DV, 2 * GLA_DK), F32),
            pltpu.VMEM((T // GLA_CHUNK, GLA_DV, 2 * GLA_DK), F32),
            pltpu.VMEM((T // GLA_CHUNK, GLA_DV, 2 * GLA_DK), BF16),
            pltpu.VMEM((T, 2 * GLA_DK), BF16),
            pltpu.VMEM((T // GLA_CHUNK, 8, 2 * GLA_DK), F32),
        ],
        name="gla_bidir",
        compiler_params=_cparams(("arbitrary", "arbitrary")),
    )(z, z, z, z, lr, zc, zc, zc, zc, lrc, lrw, lrb, norm_w.reshape(1, GLA_VW))


def _rglru_kernel(xr_ref, xrc_ref, cw_ref, cb_ref, wg_ref, bg_ref, lam_ref, r_ref, rc_ref,
                  af_scr, uf_scr, ab_scr, ub_scr, pf_scr, hf_scr, pb_scr, hb_scr, *, ctx_out):
    W = RNN_BLOCK
    a_scrs = (af_scr, ab_scr)
    u_scrs = (uf_scr, ub_scr)
    log_sig_lam = _log_sigmoid(lam_ref[...])

    def stream(x_r, out_r, h0, with_out):
        T = x_r.shape[1]
        L = T // SCAN_SEGS
        P = L + 8
        x = x_r[0].astype(F32)
        sub = lax.broadcasted_iota(jnp.int32, (8, W), 0)

        def shifted(k):
            y = pltpu.roll(x, (T - k) % T, 0)
            if k < 0:
                return jnp.concatenate([jnp.where(sub >= -k, y[0:8, :], 0.0), y[8:, :]], axis=0)
            return jnp.concatenate([y[:T - 8, :], jnp.where(sub < 8 - k, y[T - 8:, :], 0.0)], axis=0)

        xm1, xp1, xp2 = shifted(-1), shifted(1), shifted(2)
        xc = (cw_ref[0:1, :] * xm1 + cw_ref[1:2, :] * x + cw_ref[2:3, :] * xp1
              + cw_ref[3:4, :] * xp2 + cb_ref[...])
        gates = jnp.dot(xc.astype(BF16), wg_ref[0], preferred_element_type=F32) + bg_ref[0]
        for d in range(2):
            tr = jnp.tanh(gates[:, (2 * d) * W:(2 * d + 1) * W])
            ti = jnp.tanh(gates[:, (2 * d + 1) * W:(2 * d + 2) * W])
            half_c_ls = (0.5 * RGLRU_C) * log_sig_lam[d:d + 1, :]
            log_a = half_c_ls + half_c_ls * tr
            a = jnp.exp(log_a)
            th = jnp.tanh(log_a)
            q = (-0.5 * th) / (1.0 - th)
            half_m = q * lax.rsqrt(jnp.maximum(q, 1e-30))
            u = (xc * half_m) * (1.0 + ti)
            for j in range(SCAN_SEGS):
                a_scrs[d][j * P:j * P + L, :] = a[j * L:(j + 1) * L, :]
                u_scrs[d][j * P:j * P + L, :] = u[j * L:(j + 1) * L, :]

        def body(i, carry):
            hf, pf, hb, pb = carry
            rf = pl.ds(i, SCAN_SEGS, stride=P)
            a = af_scr[rf, :]
            hf = a * hf + uf_scr[rf, :]
            pf = pf * a
            hf_scr[rf, :] = hf
            pf_scr[rf, :] = pf
            rb = pl.ds(L - 1 - i, SCAN_SEGS, stride=P)
            a = ab_scr[rb, :]
            hb = a * hb + ub_scr[rb, :]
            pb = pb * a
            hb_scr[rb, :] = hb
            pb_scr[rb, :] = pb
            return hf, pf, hb, pb

        zero = jnp.zeros((SCAN_SEGS, W), F32)
        one = jnp.ones((SCAN_SEGS, W), F32)
        hf, pf, hb, pb = lax.fori_loop(0, L, body, (zero, one, zero, one), unroll=4)

        c = h0[0]
        cf = []
        for j in range(SCAN_SEGS):
            cf.append(c)
            c = hf[j:j + 1, :] + pf[j:j + 1, :] * c
        fin_f = c
        c = h0[1]
        cb = [None] * SCAN_SEGS
        for j in reversed(range(SCAN_SEGS)):
            cb[j] = c
            c = hb[j:j + 1, :] + pb[j:j + 1, :] * c
        fin_b = c
        if with_out:
            for j in range(SCAN_SEGS):
                seg = slice(j * P, j * P + L)
                h = (hf_scr[seg, :] + pf_scr[seg, :] * cf[j]) + (hb_scr[seg, :] + pb_scr[seg, :] * cb[j])
                out_r[0, j * L:(j + 1) * L, :] = h.astype(out_r.dtype)
        return fin_f, fin_b

    zrow = jnp.zeros((1, W), F32)
    states = stream(xrc_ref, rc_ref, (zrow, zrow), ctx_out)
    if not ctx_out:
        rc_ref[...] = jnp.zeros_like(rc_ref)
    stream(xr_ref, r_ref, states, True)


def _rglru(z, zc, conv_w, conv_b, wg, bg, lam, ctx_out):
    B, T, _ = z.shape
    Tc = zc.shape[1]
    G, W = RNN_BLOCKS, RNN_BLOCK
    xoff = Z_XR // W
    seg_rows = SCAN_SEGS * (T // SCAN_SEGS + 8)
    return pl.pallas_call(
        functools.partial(_rglru_kernel, ctx_out=ctx_out),
        grid=(B, G),
        in_specs=[
            pl.BlockSpec((1, T, W), lambda b, g: (b, 0, xoff + g)),
            pl.BlockSpec((1, Tc, W), lambda b, g: (b, 0, xoff + g)),
            pl.BlockSpec((4, W), lambda b, g: (0, g)),
            pl.BlockSpec((1, W), lambda b, g: (0, g)),
            pl.BlockSpec((1, W, 4 * W), lambda b, g: (g, 0, 0)),
            pl.BlockSpec((1, 1, 4 * W), lambda b, g: (g, 0, 0)),
            pl.BlockSpec((2, W), lambda b, g: (0, g)),
        ],
        out_specs=[
            pl.BlockSpec((1, T, W), lambda b, g: (b, 0, g)),
            pl.BlockSpec((1, Tc, W), lambda b, g: (b, 0, g)),
        ],
        out_shape=[
            jax.ShapeDtypeStruct((B, T, RNN_WIDTH), BF16),
            jax.ShapeDtypeStruct((B, Tc, RNN_WIDTH), BF16),
        ],
        scratch_shapes=[pltpu.VMEM((seg_rows, W), F32) for _ in range(8)],
        name="rglru_bidir",
        compiler_params=_cparams(("arbitrary", "arbitrary")),
    )(z, zc, conv_w, conv_b.reshape(1, RNN_WIDTH), wg, bg, lam)


def _mix_kernel(on_ref, h_ref, yr_ref, ga_ref, gb_ref, x_ref, g1_ref, wg_ref, wr_ref, wo_ref, o_ref):
    r = (h_ref[0].astype(F32) * _gelu_tanh(yr_ref[0].astype(F32))).astype(BF16)
    ya = jnp.dot(on_ref[0], wg_ref[...], preferred_element_type=F32) * _sigmoid(ga_ref[0].astype(F32))
    yb = jnp.dot(r, wr_ref[...], preferred_element_type=F32) * _sigmoid(gb_ref[0].astype(F32))
    y = jnp.dot((ya + yb).astype(BF16), wo_ref[...], preferred_element_type=F32)
    o_ref[0] = x_ref[0] + g1_ref[0] * y


def _mix(on, h, z, x, gate, wg, wr, wo, l):
    Bx, Tx, D = x.shape
    tm = min(MIX_TM, Tx)
    per_batch = gate.shape[0] > 1
    gmap = (lambda b, i: (b, 0, 0)) if per_batch else (lambda b, i: (0, 0, 0))
    row = lambda b, i: (b, i, 0)
    wmap = lambda b, i: (l, 0, 0)
    return pl.pallas_call(
        _mix_kernel,
        grid=(Bx, Tx // tm),
        in_specs=[
            pl.BlockSpec((1, tm, D), row),
            pl.BlockSpec((1, tm, D), row),
            pl.BlockSpec((1, tm, D), lambda b, i: (b, i, Z_YR // D_MODEL)),
            pl.BlockSpec((1, tm, D), lambda b, i: (b, i, Z_GA // D_MODEL)),
            pl.BlockSpec((1, tm, D), lambda b, i: (b, i, Z_GB // D_MODEL)),
            pl.BlockSpec((1, tm, D), row),
            pl.BlockSpec((1, 1, D), gmap),
            pl.BlockSpec((None, D, D), wmap),
            pl.BlockSpec((None, D, D), wmap),
            pl.BlockSpec((None, D, D), wmap),
        ],
        out_specs=pl.BlockSpec((1, tm, D), row),
        out_shape=jax.ShapeDtypeStruct((Bx, Tx, D), F32),
        name="mix_out",
        compiler_params=_cparams(("arbitrary", "arbitrary")),
    )(on, h, z, z, z, x, gate, wg, wr, wo)


def _ffn_down_kernel(*refs, grid_w, halo, final_norm):
    if halo:
        a_ref, up_ref, dn_ref, gv_ref = refs[:4]
        rest = refs[4:]
    else:
        a_ref, gv_ref = refs[:2]
        rest = refs[2:]
    if final_norm:
        cw_ref, cb_ref, wd_ref, x_ref, g2_ref, fw_ref, o_ref, h_scr = rest
    else:
        cw_ref, cb_ref, wd_ref, x_ref, g2_ref, o_ref, h_scr = rest
    TM = a_ref.shape[1]
    CK = FFN_CK
    i = pl.program_id(1)
    n_i = pl.num_programs(1)
    assert grid_w & (grid_w - 1) == 0
    colidx = lax.broadcasted_iota(jnp.int32, (TM, CK), 0) & (grid_w - 1)
    first_col = colidx == 0
    last_col = colidx == grid_w - 1
    if halo:
        up_scale = jnp.where(i > 0, 1.0, 0.0).astype(F32)
        dn_scale = jnp.where(i < n_i - 1, 1.0, 0.0).astype(F32)

    def conv_chunk(c):
        c0 = pl.multiple_of(c * CK, CK)
        cs = pl.ds(c0, CK)
        am = a_ref[0, :, cs].astype(F32)
        if halo:
            ext = jnp.concatenate(
                [up_ref[0, :, cs].astype(F32) * up_scale, am, dn_ref[0, :, cs].astype(F32) * dn_scale], axis=0)
            rows = [ext[kh * grid_w:kh * grid_w + TM, :] for kh in range(3)]
            v = [cw_ref[kw:kw + 1, cs] * rows[0] + cw_ref[3 + kw:4 + kw, cs] * rows[1]
                 + cw_ref[6 + kw:7 + kw, cs] * rows[2] for kw in range(3)]
        else:
            v = [cw_ref[3 + kw:4 + kw, cs] * am for kw in range(3)]
        left = jnp.where(first_col, 0.0, pltpu.roll(v[0], 1, 0))
        right = jnp.where(last_col, 0.0, pltpu.roll(v[2], TM - 1, 0))
        acc = (v[1] + cb_ref[:, cs]) + left + right
        t = jnp.tanh(acc * (_GELU_C + (_GELU_C * 0.044715) * (acc * acc)))
        h_scr[:, cs] = ((acc * gv_ref[0, :, cs].astype(F32)) * (1.0 + t)).astype(BF16)

    def chunk(c, carry):
        conv_chunk(c)
        return carry

    lax.fori_loop(0, FFN_HIDDEN // CK, chunk, 0)
    y = jnp.dot(h_scr[...], wd_ref[...], preferred_element_type=F32)
    xo = x_ref[0] + g2_ref[0] * y
    if final_norm:
        ms = jnp.mean(xo * xo, axis=-1, keepdims=True)
        xo = xo * lax.rsqrt(ms + NORM_EPS) * fw_ref[...]
    o_ref[0] = xo


def _ffn_down(u, x, gate, conv_w9, conv_b, wd, l, grid_w, final_w):
    Bx, Tx, D = x.shape
    Hd = FFN_HIDDEN
    halo = grid_w < Tx
    tm = min(FFN_TM, Tx)
    rpt = tm // grid_w if halo else 1
    n_rows = Tx // grid_w if halo else 1
    per_batch = gate.shape[0] > 1
    gmap = (lambda b, i: (b, 0, 0)) if per_batch else (lambda b, i: (0, 0, 0))
    row = lambda b, i: (b, i, 0)
    cmap = lambda b, i: (0, 0)
    in_specs = [pl.BlockSpec((1, tm, Hd), row)]
    args = [u]
    if halo:
        in_specs += [
            pl.BlockSpec((1, grid_w, Hd), lambda b, i: (b, jnp.maximum(i * rpt - 1, 0), 0)),
            pl.BlockSpec((1, grid_w, Hd), lambda b, i: (b, jnp.minimum((i + 1) * rpt, n_rows - 1), 0)),
        ]
        args += [u, u]
    in_specs += [
        pl.BlockSpec((1, tm, Hd), lambda b, i: (b, i, 1)),
        pl.BlockSpec((9, Hd), cmap),
        pl.BlockSpec((1, Hd), cmap),
        pl.BlockSpec((None, Hd, D), lambda b, i: (l, 0, 0)),
        pl.BlockSpec((1, tm, D), row),
        pl.BlockSpec((1, 1, D), gmap),
    ]
    args += [u, conv_w9, conv_b.reshape(1, Hd), wd, x, gate]
    if final_w is not None:
        in_specs.append(pl.BlockSpec((1, D), cmap))
        args.append(final_w.reshape(1, D))
    return pl.pallas_call(
        functools.partial(_ffn_down_kernel, grid_w=grid_w, halo=halo, final_norm=final_w is not None),
        grid=(Bx, Tx // tm),
        in_specs=in_specs,
        out_specs=pl.BlockSpec((1, tm, D), row),
        out_shape=jax.ShapeDtypeStruct((Bx, Tx, D), F32),
        scratch_shapes=[pltpu.VMEM((tm, Hd), BF16)],
        name="ffn_conv_down",
        compiler_params=_cparams(("arbitrary", "arbitrary")),
    )(*args)


def kernel(x, c, ctx, c_ctx, ada_w, ada_b, norm1_w, w_in, gla_lr_w, gla_lr_b, gla_norm_w, rnn_conv_w,
           rnn_conv_b, rnn_wa, rnn_ba, rnn_wx, rnn_bx, rnn_lambda, w_gla_o, w_rnn_o, w_out, norm2_w,
           ffn_up, ffn_conv_w, ffn_conv_b, ffn_down, final_norm_w):
    B, T, D = x.shape
    Tc = ctx.shape[1]
    depth = w_in.shape[0]

    cc = jnp.zeros((16, D), F32).at[:B].set(c).at[B].set(c_ctx)
    mod = _ada(cc, ada_w, ada_b)
    ctxf = ctx.reshape(1, B * Tc, D)

    lr_end = LR_START + 2 * GLA_LOWRANK
    w_z = jnp.concatenate([w_in[:, :, :LR_START], w_in[:, :, lr_end:]], axis=2).astype(BF16)
    w_lr = jnp.pad(w_in[:, :, LR_START:lr_end], ((0, 0), (0, 0), (0, LANES - 2 * GLA_LOWRANK))).astype(BF16)
    wg_o, wr_o, wo = w_gla_o.astype(BF16), w_rnn_o.astype(BF16), w_out.astype(BF16)
    w_up, w_dn = ffn_up.astype(BF16), (0.5 * ffn_down).astype(BF16)

    for l in range(depth):
        last = l == depth - 1
        mx = mod[l, :B].reshape(B, 1, 6 * D)
        mc = mod[l, B:B + 1].reshape(1, 1, 6 * D)
        sh1, sc1, g1, sh2, sc2, g2 = [mx[:, :, k * D:(k + 1) * D] for k in range(6)]
        csh1, csc1, cg1, csh2, csc2, cg2 = [mc[:, :, k * D:(k + 1) * D] for k in range(6)]

        lrw = jnp.zeros((2, LANES, GLA_KW), F32)
        lrw = lrw.at[0, :GLA_LOWRANK].set(gla_lr_w[l, 0]).at[1, GLA_LOWRANK:2 * GLA_LOWRANK].set(gla_lr_w[l, 1])
        lrw = lrw.astype(BF16)
        wgate = (0.5 * jnp.concatenate([rnn_wa[l, 0], rnn_wx[l, 0], rnn_wa[l, 1], rnn_wx[l, 1]], axis=-1)
                 ).astype(BF16)
        bgate = 0.5 * jnp.concatenate(
            [v.reshape(RNN_BLOCKS, 1, RNN_BLOCK) for v in (rnn_ba[l, 0], rnn_bx[l, 0], rnn_ba[l, 1], rnn_bx[l, 1])],
            axis=-1)
        conv9 = ffn_conv_w[l].reshape(9, FFN_HIDDEN)

        z, lr = _proj(x, norm1_w[l], sh1, sc1, w_z, w_lr, l, PROJ_TN_IN)
        zc, lrc = _proj(ctxf, norm1_w[l], csh1, csc1, w_z, w_lr, l, PROJ_TN_IN)
        zc = zc.reshape(B, Tc, Z_WIDTH)
        lrc = lrc.reshape(B, Tc, LANES)
        on, onc = _gla(z, lr, zc, lrc, lrw, gla_lr_b[l], gla_norm_w[l], ctx_out=not last)
        r, rc = _rglru(z, zc, rnn_conv_w[l], rnn_conv_b[l], wgate, bgate, rnn_lambda[l], ctx_out=not last)
        x = _mix(on, r, z, x, g1, wg_o, wr_o, wo, l)

        u = _proj(x, norm2_w[l], sh2, sc2, w_up, None, l, PROJ_TN_UP)
        x = _ffn_down(u, x, g2, conv9, ffn_conv_b[l], w_dn, l, GRID_W, final_norm_w if last else None)

        if not last:
            ctxf = _mix(onc.reshape(1, B * Tc, D), rc.reshape(1, B * Tc, D), zc.reshape(1, B * Tc, Z_WIDTH),
                        ctxf, cg1, wg_o, wr_o, wo, l)
            uc = _proj(ctxf, norm2_w[l], csh2, csc2, w_up, None, l, PROJ_TN_UP)
            ctxf = _ffn_down(uc.reshape(B, Tc, 2 * FFN_HIDDEN), ctxf.reshape(B, Tc, D), cg2, conv9,
                             ffn_conv_b[l], w_dn, l, Tc, None).reshape(1, B * Tc, D)
    return x
```

```python
import functools

import jax
import jax.numpy as jnp
from jax import lax
from jax.experimental import pallas as pl
from jax.experimental.pallas import tpu as pltpu

F32 = jnp.float32
BF16 = jnp.bfloat16

D_MODEL = 1024
GRID_W = 64
GLA_HEADS = 4
GLA_DK = 128
GLA_DV = 256
GLA_KW = GLA_HEADS * GLA_DK
GLA_VW = GLA_HEADS * GLA_DV
GLA_LOWRANK = 16
GLA_TAU = 16.0
GLA_CHUNK = 64
GLA_GROUP = 256
RNN_WIDTH = 1024
RNN_BLOCKS = 8
RNN_BLOCK = RNN_WIDTH // RNN_BLOCKS
RGLRU_C = 8.0
FFN_HIDDEN = 2816
NORM_EPS = 1e-6

LANES = 128
SCAN_SEGS = 8
VMEM_LIMIT = 56 * 1024 * 1024

Z_Q, Z_K, Z_V, Z_G, Z_XR, Z_YR, Z_GA, Z_GB = 0, 512, 1024, 2048, 3072, 4096, 5120, 6144
Z_WIDTH = 7168
LR_START = 2 * GLA_KW + 2 * GLA_VW
PROJ_TM = 1024
PROJ_TN_IN = 1792
PROJ_TN_UP = 2816
MIX_TM = 512
FFN_TM = 512
FFN_CK = 256


def _cparams(sem):
    return pltpu.CompilerParams(dimension_semantics=sem, vmem_limit_bytes=VMEM_LIMIT)


def _sigmoid(x):
    return 0.5 + 0.5 * jnp.tanh(0.5 * x)


def _silu(x):
    return x * _sigmoid(x)


_GELU_C = 0.7978845608028654


def _gelu_tanh(x):
    return 0.5 * x * (1.0 + jnp.tanh(_GELU_C * (x + 0.044715 * (x * x * x))))


def _log_sigmoid(x):
    return jnp.minimum(x, 0.0) - jnp.log1p(jnp.exp(-jnp.abs(x)))


def _ada_kernel(c_ref, w_ref, b_ref, o_ref):
    s = _silu(c_ref[...])
    o_ref[0] = jnp.dot(s.astype(BF16), w_ref[0].astype(BF16), preferred_element_type=F32) + b_ref[0]


def _ada(cc, ada_w, ada_b):
    L, D, N = ada_w.shape
    tn = 1536
    return pl.pallas_call(
        _ada_kernel,
        grid=(L, N // tn),
        in_specs=[
            pl.BlockSpec((16, D), lambda l, n: (0, 0)),
            pl.BlockSpec((1, D, tn), lambda l, n: (l, 0, n)),
            pl.BlockSpec((1, 1, tn), lambda l, n: (l, 0, n)),
        ],
        out_specs=pl.BlockSpec((1, 16, tn), lambda l, n: (l, 0, n)),
        out_shape=jax.ShapeDtypeStruct((L, 16, N), F32),
        name="ada_mod",
        compiler_params=_cparams(("arbitrary", "arbitrary")),
    )(cc, ada_w, ada_b.reshape(L, 1, N))


def _proj_kernel(*refs, has_lr):
    if has_lr:
        x_ref, nw_ref, sh_ref, sc_ref, w_ref, wlr_ref, o_ref, olr_ref, xn_scr = refs
    else:
        x_ref, nw_ref, sh_ref, sc_ref, w_ref, o_ref, xn_scr = refs
    n = pl.program_id(2)

    @pl.when(n == 0)
    def _():
        x = x_ref[0]
        ms = jnp.mean(x * x, axis=-1, keepdims=True)
        y = x * lax.rsqrt(ms + NORM_EPS) * nw_ref[...]
        y = y * (1.0 + sc_ref[0]) + sh_ref[0]
        xb = y.astype(BF16)
        xn_scr[...] = xb
        if has_lr:
            olr_ref[0] = jnp.dot(xb, wlr_ref[...], preferred_element_type=F32).astype(olr_ref.dtype)

    o_ref[0] = jnp.dot(xn_scr[...], w_ref[...], preferred_element_type=F32).astype(o_ref.dtype)


def _proj(x, norm_w, shift, scale, w, wlr, l, tn):
    Bx, Tx, D = x.shape
    N = w.shape[2]
    tm = min(PROJ_TM, Tx)
    per_batch = shift.shape[0] > 1
    mod_map = (lambda b, i, n: (b, 0, 0)) if per_batch else (lambda b, i, n: (0, 0, 0))
    has_lr = wlr is not None
    in_specs = [
        pl.BlockSpec((1, tm, D), lambda b, i, n: (b, i, 0)),
        pl.BlockSpec((1, D), lambda b, i, n: (0, 0)),
        pl.BlockSpec((1, 1, D), mod_map),
        pl.BlockSpec((1, 1, D), mod_map),
        pl.BlockSpec((None, D, tn), lambda b, i, n: (l, 0, n)),
    ]
    args = [x, norm_w.reshape(1, D), shift, scale, w]
    out_specs = [pl.BlockSpec((1, tm, tn), lambda b, i, n: (b, i, n))]
    out_shape = [jax.ShapeDtypeStruct((Bx, Tx, N), BF16)]
    if has_lr:
        in_specs.append(pl.BlockSpec((None, D, LANES), lambda b, i, n: (l, 0, 0)))
        args.append(wlr)
        out_specs.append(pl.BlockSpec((1, tm, LANES), lambda b, i, n: (b, i, 0)))
        out_shape.append(jax.ShapeDtypeStruct((Bx, Tx, LANES), BF16))
    res = pl.pallas_call(
        functools.partial(_proj_kernel, has_lr=has_lr),
        grid=(Bx, Tx // tm, N // tn),
        in_specs=in_specs,
        out_specs=out_specs,
        out_shape=out_shape,
        scratch_shapes=[pltpu.VMEM((tm, D), BF16)],
        name="norm_mod_proj",
        compiler_params=_cparams(("arbitrary", "arbitrary", "arbitrary")),
    )(*args)
    return res if has_lr else res[0]


_NT = (((1,), (1,)), ((), ()))
_TN = (((0,), (0,)), ((), ()))


def _gla_kernel(q_ref, k_ref, v_ref, g_ref, lr_ref, qc_ref, kc_ref, vc_ref, gc_ref, lrc_ref,
                lrw_ref, lrb_ref, nw_ref, o_ref, oc_ref,
                la_scr, oacc_scr, s_scr, kv_scr, sall_scr, qd_scr, dec_scr, *, ctx_out):
    C, G = GLA_CHUNK, GLA_GROUP
    CPG = G // C
    row = lax.broadcasted_iota(jnp.int32, (G, G), 0)
    col = lax.broadcasted_iota(jnp.int32, (G, G), 1)
    assert C & (C - 1) == 0
    same = (row ^ col) < C
    masks = (same & (col <= row), same & (col >= row))
    tris = tuple(jnp.where(m, 1.0, 0.0).astype(BF16) for m in masks)
    qscale = GLA_DK ** -0.5

    DK = GLA_DK
    FWD, BWD = slice(0, DK), slice(DK, 2 * DK)

    def run(q_r, k_r, v_r, g_r, lr_r, out_r, T, with_out):
        NG = T // G
        NC = T // C
        lrv = lr_r[0]
        for d in range(2):
            xx = jnp.dot(lrv, lrw_ref[d], preferred_element_type=F32) + lrb_ref[d:d + 1, :]
            la_scr[d, 0:T, :] = _log_sigmoid(xx) * (1.0 / GLA_TAU)

        def group(gi, carry):
            r0 = pl.multiple_of(gi * G, G)
            rows = pl.ds(r0, G)
            hls = []
            for d in range(2):
                la = la_scr[d, rows, :]
                hi = la.astype(BF16)
                hls += [hi, (la - hi.astype(F32)).astype(BF16)]
            kf = k_r[0, rows, :].astype(F32)
            vb = v_r[0, rows, :]
            if with_out:
                qf = q_r[0, rows, :].astype(F32) * qscale
            kds, qds, bls, comb = [], [], [], None
            for d in range(2):
                hl = jnp.concatenate(hls[2 * d:2 * d + 2], axis=1)
                cs = jnp.dot(tris[d], hl, preferred_element_type=F32)
                b = cs[:, :DK] + cs[:, DK:]
                ends = [(c + 1) * C - 1 if d == 0 else c * C for c in range(CPG)]
                bl = jnp.concatenate([jnp.broadcast_to(b[e:e + 1, :], (C, DK)) for e in ends], axis=0)
                bls.append(bl)
                kds.append((kf * jnp.exp(bl - b)).astype(BF16))
                if with_out:
                    qd = (qf * jnp.exp(b)).astype(BF16)
                    ki = (kf * jnp.exp(-b)).astype(BF16)
                    sc = jnp.where(masks[d], lax.dot_general(qd, ki, _NT, preferred_element_type=F32), 0.0)
                    comb = sc if comb is None else comb + sc
                    qds.append(qd)
            if with_out:
                oacc_scr[rows, :] = jnp.dot(comb.astype(BF16), vb, preferred_element_type=F32)
                qd_scr[rows, :] = jnp.concatenate(qds, axis=1)
            kd = jnp.concatenate(kds, axis=1)
            dec = jnp.exp(jnp.concatenate(bls, axis=1))
            for c in range(CPG):
                cr = slice(c * C, (c + 1) * C)
                n = gi * CPG + c
                kv_scr[n] = lax.dot_general(vb[cr, :], kd[cr, :], _TN, preferred_element_type=F32)
                dec_scr[n] = dec[c * C:c * C + 8, :]
            return carry

        lax.fori_loop(0, NG, group, 0, unroll=4)

        def scan(n, st):
            sf, sb = st
            nb = NC - 1 - n
            sall_scr[n, :, FWD] = sf.astype(BF16)
            sall_scr[nb, :, BWD] = sb.astype(BF16)
            sf = sf * dec_scr[n, 0:1, FWD] + kv_scr[n, :, FWD]
            sb = sb * dec_scr[nb, 0:1, BWD] + kv_scr[nb, :, BWD]
            return sf, sb

        sf, sb = lax.fori_loop(0, NC, scan, (s_scr[:, FWD], s_scr[:, BWD]), unroll=2)
        s_scr[:, FWD] = sf
        s_scr[:, BWD] = sb

        if with_out:
            def inter(gi, carry):
                for c in range(CPG):
                    n = gi * CPG + c
                    rows = pl.ds(pl.multiple_of(n * C, C), C)
                    oe = lax.dot_general(qd_scr[rows, :], sall_scr[n], _NT, preferred_element_type=F32)
                    oacc_scr[rows, :] = oacc_scr[rows, :] + oe
                return carry

            lax.fori_loop(0, NG, inter, 0, unroll=2)

            o = oacc_scr[0:T, :]
            inv = lax.rsqrt(jnp.mean(o * o, axis=-1, keepdims=True) + NORM_EPS)
            out_r[0] = (o * inv * nw_ref[...] * _silu(g_r[0].astype(F32))).astype(out_r.dtype)

    Tc = qc_ref.shape[1]
    T = q_ref.shape[1]
    s_scr[...] = jnp.zeros_like(s_scr)
    run(qc_ref, kc_ref, vc_ref, gc_ref, lrc_ref, oc_ref, Tc, ctx_out)
    if not ctx_out:
        oc_ref[...] = jnp.zeros_like(oc_ref)
    run(q_ref, k_ref, v_ref, g_ref, lr_ref, o_ref, T, True)


def _gla(z, lr, zc, lrc, lrw, lrb, norm_w, ctx_out):
    B, T, _ = z.shape
    Tc = zc.shape[1]
    H = GLA_HEADS

    def specs(Tx):
        return [
            pl.BlockSpec((1, Tx, GLA_DK), lambda b, h: (b, 0, Z_Q // GLA_DK + h)),
            pl.BlockSpec((1, Tx, GLA_DK), lambda b, h: (b, 0, Z_K // GLA_DK + h)),
            pl.BlockSpec((1, Tx, GLA_DV), lambda b, h: (b, 0, Z_V // GLA_DV + h)),
            pl.BlockSpec((1, Tx, GLA_DV), lambda b, h: (b, 0, Z_G // GLA_DV + h)),
            pl.BlockSpec((1, Tx, LANES), lambda b, h: (b, 0, 0)),
        ]

    in_specs = specs(T) + specs(Tc) + [
        pl.BlockSpec((2, LANES, GLA_DK), lambda b, h: (0, 0, h)),
        pl.BlockSpec((2, GLA_DK), lambda b, h: (0, h)),
        pl.BlockSpec((1, GLA_DV), lambda b, h: (0, h)),
    ]
    return pl.pallas_call(
        functools.partial(_gla_kernel, ctx_out=ctx_out),
        grid=(B, H),
        in_specs=in_specs,
        out_specs=[
            pl.BlockSpec((1, T, GLA_DV), lambda b, h: (b, 0, h)),
            pl.BlockSpec((1, Tc, GLA_DV), lambda b, h: (b, 0, h)),
        ],
        out_shape=[
            jax.ShapeDtypeStruct((B, T, GLA_VW), BF16),
            jax.ShapeDtypeStruct((B, Tc, GLA_VW), BF16),
        ],
        scratch_shapes=[
            pltpu.VMEM((2, T, GLA_DK), F32),
            pltpu.VMEM((T, GLA_DV), F32),
            pltpu.VMEM((GLA_DV, 2 * GLA_DK), F32),
            pltpu.VMEM((T // GLA_CHUNK, GLA_DV, 2 * GLA_DK), F32),
            pltpu.VMEM((T // GLA_CHUNK, GLA_DV, 2 * GLA_DK), BF16),
            pltpu.VMEM((T, 2 * GLA_DK), BF16),
            pltpu.VMEM((T // GLA_CHUNK, 8, 2 * GLA_DK), F32),
        ],
        name="gla_bidir",
        compiler_params=_cparams(("arbitrary", "arbitrary")),
    )(z, z, z, z, lr, zc, zc, zc, zc, lrc, lrw, lrb, norm_w.reshape(1, GLA_VW))


def _rglru_kernel(xr_ref, xrc_ref, cw_ref, cb_ref, wg_ref, bg_ref, lam_ref, r_ref, rc_ref,
                  af_scr, uf_scr, ab_scr, ub_scr, pf_scr, hf_scr, pb_scr, hb_scr, *, ctx_out):
    W = RNN_BLOCK
    a_scrs = (af_scr, ab_scr)
    u_scrs = (uf_scr, ub_scr)
    log_sig_lam = _log_sigmoid(lam_ref[...])

    def stream(x_r, out_r, h0, with_out):
        T = x_r.shape[1]
        L = T // SCAN_SEGS
        P = L + 8
        x = x_r[0].astype(F32)
        sub = lax.broadcasted_iota(jnp.int32, (8, W), 0)

        def shifted(k):
            y = pltpu.roll(x, (T - k) % T, 0)
            if k < 0:
                return jnp.concatenate([jnp.where(sub >= -k, y[0:8, :], 0.0), y[8:, :]], axis=0)
            return jnp.concatenate([y[:T - 8, :], jnp.where(sub < 8 - k, y[T - 8:, :], 0.0)], axis=0)

        xm1, xp1, xp2 = shifted(-1), shifted(1), shifted(2)
        xc = (cw_ref[0:1, :] * xm1 + cw_ref[1:2, :] * x + cw_ref[2:3, :] * xp1
              + cw_ref[3:4, :] * xp2 + cb_ref[...])
        gates = jnp.dot(xc.astype(BF16), wg_ref[0], preferred_element_type=F32) + bg_ref[0]
        for d in range(2):
            tr = jnp.tanh(gates[:, (2 * d) * W:(2 * d + 1) * W])
            ti = jnp.tanh(gates[:, (2 * d + 1) * W:(2 * d + 2) * W])
            half_c_ls = (0.5 * RGLRU_C) * log_sig_lam[d:d + 1, :]
            log_a = half_c_ls + half_c_ls * tr
            a = jnp.exp(log_a)
            th = jnp.tanh(log_a)
            q = (-0.5 * th) / (1.0 - th)
            half_m = q * lax.rsqrt(jnp.maximum(q, 1e-30))
            u = (xc * half_m) * (1.0 + ti)
            for j in range(SCAN_SEGS):
                a_scrs[d][j * P:j * P + L, :] = a[j * L:(j + 1) * L, :]
                u_scrs[d][j * P:j * P + L, :] = u[j * L:(j + 1) * L, :]

        def body(i, carry):
            hf, pf, hb, pb = carry
            rf = pl.ds(i, SCAN_SEGS, stride=P)
            a = af_scr[rf, :]
            hf = a * hf + uf_scr[rf, :]
            pf = pf * a
            hf_scr[rf, :] = hf
            pf_scr[rf, :] = pf
            rb = pl.ds(L - 1 - i, SCAN_SEGS, stride=P)
            a = ab_scr[rb, :]
            hb = a * hb + ub_scr[rb, :]
            pb = pb * a
            hb_scr[rb, :] = hb
            pb_scr[rb, :] = pb
            return hf, pf, hb, pb

        zero = jnp.zeros((SCAN_SEGS, W), F32)
        one = jnp.ones((SCAN_SEGS, W), F32)
        hf, pf, hb, pb = lax.fori_loop(0, L, body, (zero, one, zero, one), unroll=4)

        c = h0[0]
        cf = []
        for j in range(SCAN_SEGS):
            cf.append(c)
            c = hf[j:j + 1, :] + pf[j:j + 1, :] * c
        fin_f = c
        c = h0[1]
        cb = [None] * SCAN_SEGS
        for j in reversed(range(SCAN_SEGS)):
            cb[j] = c
            c = hb[j:j + 1, :] + pb[j:j + 1, :] * c
        fin_b = c
        if with_out:
            for j in range(SCAN_SEGS):
                seg = slice(j * P, j * P + L)
                h = (hf_scr[seg, :] + pf_scr[seg, :] * cf[j]) + (hb_scr[seg, :] + pb_scr[seg, :] * cb[j])
                out_r[0, j * L:(j + 1) * L, :] = h.astype(out_r.dtype)
        return fin_f, fin_b

    zrow = jnp.zeros((1, W), F32)
    states = stream(xrc_ref, rc_ref, (zrow, zrow), ctx_out)
    if not ctx_out:
        rc_ref[...] = jnp.zeros_like(rc_ref)
    stream(xr_ref, r_ref, states, True)


def _rglru(z, zc, conv_w, conv_b, wg, bg, lam, ctx_out):
    B, T, _ = z.shape
    Tc = zc.shape[1]
    G, W = RNN_BLOCKS, RNN_BLOCK
    xoff = Z_XR // W
    seg_rows = SCAN_SEGS * (T // SCAN_SEGS + 8)
    return pl.pallas_call(
        functools.partial(_rglru_kernel, ctx_out=ctx_out),
        grid=(B, G),
        in_specs=[
            pl.BlockSpec((1, T, W), lambda b, g: (b, 0, xoff + g)),
            pl.BlockSpec((1, Tc, W), lambda b, g: (b, 0, xoff + g)),
            pl.BlockSpec((4, W), lambda b, g: (0, g)),
            pl.BlockSpec((1, W), lambda b, g: (0, g)),
            pl.BlockSpec((1, W, 4 * W), lambda b, g: (g, 0, 0)),
            pl.BlockSpec((1, 1, 4 * W), lambda b, g: (g, 0, 0)),
            pl.BlockSpec((2, W), lambda b, g: (0, g)),
        ],
        out_specs=[
            pl.BlockSpec((1, T, W), lambda b, g: (b, 0, g)),
            pl.BlockSpec((1, Tc, W), lambda b, g: (b, 0, g)),
        ],
        out_shape=[
            jax.ShapeDtypeStruct((B, T, RNN_WIDTH), BF16),
            jax.ShapeDtypeStruct((B, Tc, RNN_WIDTH), BF16),
        ],
        scratch_shapes=[pltpu.VMEM((seg_rows, W), F32) for _ in range(8)],
        name="rglru_bidir",
        compiler_params=_cparams(("arbitrary", "arbitrary")),
    )(z, zc, conv_w, conv_b.reshape(1, RNN_WIDTH), wg, bg, lam)


def _mix_kernel(on_ref, h_ref, yr_ref, ga_ref, gb_ref, x_ref, g1_ref, wg_ref, wr_ref, wo_ref, o_ref):
    r = (h_ref[0].astype(F32) * _gelu_tanh(yr_ref[0].astype(F32))).astype(BF16)
    ya = jnp.dot(on_ref[0], wg_ref[...], preferred_element_type=F32) * _sigmoid(ga_ref[0].astype(F32))
    yb = jnp.dot(r, wr_ref[...], preferred_element_type=F32) * _sigmoid(gb_ref[0].astype(F32))
    y = jnp.dot((ya + yb).astype(BF16), wo_ref[...], preferred_element_type=F32)
    o_ref[0] = x_ref[0] + g1_ref[0] * y


def _mix(on, h, z, x, gate, wg, wr, wo, l):
    Bx, Tx, D = x.shape
    tm = min(MIX_TM, Tx)
    per_batch = gate.shape[0] > 1
    gmap = (lambda b, i: (b, 0, 0)) if per_batch else (lambda b, i: (0, 0, 0))
    row = lambda b, i: (b, i, 0)
    wmap = lambda b, i: (l, 0, 0)
    return pl.pallas_call(
        _mix_kernel,
        grid=(Bx, Tx // tm),
        in_specs=[
            pl.BlockSpec((1, tm, D), row),
            pl.BlockSpec((1, tm, D), row),
            pl.BlockSpec((1, tm, D), lambda b, i: (b, i, Z_YR // D_MODEL)),
            pl.BlockSpec((1, tm, D), lambda b, i: (b, i, Z_GA // D_MODEL)),
            pl.BlockSpec((1, tm, D), lambda b, i: (b, i, Z_GB // D_MODEL)),
            pl.BlockSpec((1, tm, D), row),
            pl.BlockSpec((1, 1, D), gmap),
            pl.BlockSpec((None, D, D), wmap),
            pl.BlockSpec((None, D, D), wmap),
            pl.BlockSpec((None, D, D), wmap),
        ],
        out_specs=pl.BlockSpec((1, tm, D), row),
        out_shape=jax.ShapeDtypeStruct((Bx, Tx, D), F32),
        name="mix_out",
        compiler_params=_cparams(("arbitrary", "arbitrary")),
    )(on, h, z, z, z, x, gate, wg, wr, wo)


def _ffn_down_kernel(*refs, grid_w, halo, final_norm):
    if halo:
        a_ref, up_ref, dn_ref, gv_ref = refs[:4]
        rest = refs[4:]
    else:
        a_ref, gv_ref = refs[:2]
        rest = refs[2:]
    if final_norm:
        cw_ref, cb_ref, wd_ref, x_ref, g2_ref, fw_ref, o_ref, h_scr = rest
    else:
        cw_ref, cb_ref, wd_ref, x_ref, g2_ref, o_ref, h_scr = rest
    TM = a_ref.shape[1]
    CK = FFN_CK
    i = pl.program_id(1)
    n_i = pl.num_programs(1)
    assert grid_w & (grid_w - 1) == 0
    colidx = lax.broadcasted_iota(jnp.int32, (TM, CK), 0) & (grid_w - 1)
    first_col = colidx == 0
    last_col = colidx == grid_w - 1
    if halo:
        up_scale = jnp.where(i > 0, 1.0, 0.0).astype(F32)
        dn_scale = jnp.where(i < n_i - 1, 1.0, 0.0).astype(F32)

    def conv_chunk(c):
        c0 = pl.multiple_of(c * CK, CK)
        cs = pl.ds(c0, CK)
        am = a_ref[0, :, cs].astype(F32)
        if halo:
            ext = jnp.concatenate(
                [up_ref[0, :, cs].astype(F32) * up_scale, am, dn_ref[0, :, cs].astype(F32) * dn_scale], axis=0)
            rows = [ext[kh * grid_w:kh * grid_w + TM, :] for kh in range(3)]
            v = [cw_ref[kw:kw + 1, cs] * rows[0] + cw_ref[3 + kw:4 + kw, cs] * rows[1]
                 + cw_ref[6 + kw:7 + kw, cs] * rows[2] for kw in range(3)]
        else:
            v = [cw_ref[3 + kw:4 + kw, cs] * am for kw in range(3)]
        left = jnp.where(first_col, 0.0, pltpu.roll(v[0], 1, 0))
        right = jnp.where(last_col, 0.0, pltpu.roll(v[2], TM - 1, 0))
        acc = (v[1] + cb_ref[:, cs]) + left + right
        t = jnp.tanh(acc * (_GELU_C + (_GELU_C * 0.044715) * (acc * acc)))
        h_scr[:, cs] = ((acc * gv_ref[0, :, cs].astype(F32)) * (1.0 + t)).astype(BF16)

    def chunk(c, carry):
        conv_chunk(c)
        return carry

    lax.fori_loop(0, FFN_HIDDEN // CK, chunk, 0)
    y = jnp.dot(h_scr[...], wd_ref[...], preferred_element_type=F32)
    xo = x_ref[0] + g2_ref[0] * y
    if final_norm:
        ms = jnp.mean(xo * xo, axis=-1, keepdims=True)
        xo = xo * lax.rsqrt(ms + NORM_EPS) * fw_ref[...]
    o_ref[0] = xo


def _ffn_down(u, x, gate, conv_w9, conv_b, wd, l, grid_w, final_w):
    Bx, Tx, D = x.shape
    Hd = FFN_HIDDEN
    halo = grid_w < Tx
    tm = min(FFN_TM, Tx)
    rpt = tm // grid_w if halo else 1
    n_rows = Tx // grid_w if halo else 1
    per_batch = gate.shape[0] > 1
    gmap = (lambda b, i: (b, 0, 0)) if per_batch else (lambda b, i: (0, 0, 0))
    row = lambda b, i: (b, i, 0)
    cmap = lambda b, i: (0, 0)
    in_specs = [pl.BlockSpec((1, tm, Hd), row)]
    args = [u]
    if halo:
        in_specs += [
            pl.BlockSpec((1, grid_w, Hd), lambda b, i: (b, jnp.maximum(i * rpt - 1, 0), 0)),
            pl.BlockSpec((1, grid_w, Hd), lambda b, i: (b, jnp.minimum((i + 1) * rpt, n_rows - 1), 0)),
        ]
        args += [u, u]
    in_specs += [
        pl.BlockSpec((1, tm, Hd), lambda b, i: (b, i, 1)),
        pl.BlockSpec((9, Hd), cmap),
        pl.BlockSpec((1, Hd), cmap),
        pl.BlockSpec((None, Hd, D), lambda b, i: (l, 0, 0)),
        pl.BlockSpec((1, tm, D), row),
        pl.BlockSpec((1, 1, D), gmap),
    ]
    args += [u, conv_w9, conv_b.reshape(1, Hd), wd, x, gate]
    if final_w is not None:
        in_specs.append(pl.BlockSpec((1, D), cmap))
        args.append(final_w.reshape(1, D))
    return pl.pallas_call(
        functools.partial(_ffn_down_kernel, grid_w=grid_w, halo=halo, final_norm=final_w is not None),
        grid=(Bx, Tx // tm),
        in_specs=in_specs,
        out_specs=pl.BlockSpec((1, tm, D), row),
        out_shape=jax.ShapeDtypeStruct((Bx, Tx, D), F32),
        scratch_shapes=[pltpu.VMEM((tm, Hd), BF16)],
        name="ffn_conv_down",
        compiler_params=_cparams(("arbitrary", "arbitrary")),
    )(*args)


def _ffn_kernel(*refs, grid_w, halo, final_norm, tiles_per_batch, n_tiles):
    if halo:
        x_ref, xup_ref, xdn_ref = refs[:3]
        refs = refs[3:]
    else:
        x_ref = refs[0]
        refs = refs[1:]
    nw_ref, sh_ref, sc_ref, wup_ref, cw_ref, cb_ref, wd_ref, xp_ref, g2_ref = refs[:9]
    refs = refs[9:]
    if final_norm:
        fw_ref, o_ref, xn_scr, h_scr = refs
    else:
        o_ref, xn_scr, h_scr = refs
    TM = x_ref.shape[1]
    Hd = FFN_HIDDEN
    CK = FFN_CK
    g = pl.program_id(0)
    slot = g % 2
    ti = jnp.minimum(g, n_tiles - 1) % tiles_per_batch

    @pl.when(g == 0)
    def _():
        h_scr[1] = jnp.zeros(h_scr.shape[1:], h_scr.dtype)

    if halo:
        xe = jnp.concatenate([xup_ref[0], x_ref[0], xdn_ref[0]], axis=0)
    else:
        xe = x_ref[0]
    ms = jnp.mean(xe * xe, axis=-1, keepdims=True)
    xn = xe * lax.rsqrt(ms + NORM_EPS) * nw_ref[...]
    xn_scr[...] = (xn * (1.0 + sc_ref[0]) + sh_ref[0]).astype(BF16)

    y = jnp.dot(h_scr[1 - slot], wd_ref[...], preferred_element_type=F32)
    xo = xp_ref[0] + g2_ref[0] * y
    if final_norm:
        ms = jnp.mean(xo * xo, axis=-1, keepdims=True)
        xo = xo * lax.rsqrt(ms + NORM_EPS) * fw_ref[...]
    o_ref[0] = xo

    assert grid_w & (grid_w - 1) == 0
    colidx = lax.broadcasted_iota(jnp.int32, (TM, CK), 0) & (grid_w - 1)
    first_col = colidx == 0
    last_col = colidx == grid_w - 1
    if halo:
        up_scale = jnp.where(ti > 0, 1.0, 0.0).astype(F32)
        dn_scale = jnp.where(ti < tiles_per_batch - 1, 1.0, 0.0).astype(F32)
        xn_main = xn_scr[grid_w:grid_w + TM, :]
    else:
        xn_main = xn_scr[...]
    for c in range(Hd // CK):
        cs = slice(c * CK, (c + 1) * CK)
        a = jnp.dot(xn_scr[...], wup_ref[:, cs], preferred_element_type=F32)
        gv = jnp.dot(xn_main, wup_ref[:, Hd + c * CK:Hd + (c + 1) * CK], preferred_element_type=F32)
        if halo:
            rows = [a[0:grid_w, :] * up_scale, a[grid_w:grid_w + TM, :], a[grid_w + TM:, :] * dn_scale]
            ext = jnp.concatenate(rows, axis=0)
            rows = [ext[kh * grid_w:kh * grid_w + TM, :] for kh in range(3)]
            v = [cw_ref[kw:kw + 1, cs] * rows[0] + cw_ref[3 + kw:4 + kw, cs] * rows[1]
                 + cw_ref[6 + kw:7 + kw, cs] * rows[2] for kw in range(3)]
        else:
            v = [cw_ref[3 + kw:4 + kw, cs] * a for kw in range(3)]
        left = jnp.where(first_col, 0.0, pltpu.roll(v[0], 1, 0))
        right = jnp.where(last_col, 0.0, pltpu.roll(v[2], TM - 1, 0))
        acc = (v[1] + cb_ref[:, cs]) + left + right
        t = jnp.tanh(acc * (_GELU_C + (_GELU_C * 0.044715) * (acc * acc)))
        h_scr[slot, :, cs] = ((acc * gv) * (1.0 + t)).astype(BF16)


def _ffn(x, norm_w, shift, scale, gate, w_up, conv_w9, conv_b, wd, l, grid_w, final_w):
    Bx, Tx, D = x.shape
    Hd = FFN_HIDDEN
    halo = grid_w < Tx
    tm = min(FFN_TM, Tx)
    tpb = Tx // tm
    n_tiles = Bx * tpb
    rpt = tm // grid_w if halo else 1
    n_rows = Tx // grid_w if halo else 1
    cur = lambda g: jnp.minimum(g, n_tiles - 1)
    prev = lambda g: jnp.maximum(g - 1, 0)
    per_batch = gate.shape[0] > 1
    mod_cur = (lambda g: (cur(g) // tpb, 0, 0)) if per_batch else (lambda g: (0, 0, 0))
    mod_prev = (lambda g: (prev(g) // tpb, 0, 0)) if per_batch else (lambda g: (0, 0, 0))
    const2 = lambda g: (0, 0)
    resident = dict(pipeline_mode=pl.Buffered(1))
    in_specs = [pl.BlockSpec((1, tm, D), lambda g: (cur(g) // tpb, cur(g) % tpb, 0))]
    args = [x]
    if halo:
        in_specs += [
            pl.BlockSpec((1, grid_w, D), lambda g: (cur(g) // tpb, jnp.maximum((cur(g) % tpb) * rpt - 1, 0), 0)),
            pl.BlockSpec((1, grid_w, D),
                         lambda g: (cur(g) // tpb, jnp.minimum((cur(g) % tpb + 1) * rpt, n_rows - 1), 0)),
        ]
        args += [x, x]
    in_specs += [
        pl.BlockSpec((1, D), const2),
        pl.BlockSpec((1, 1, D), mod_cur),
        pl.BlockSpec((1, 1, D), mod_cur),
        pl.BlockSpec((None, D, 2 * Hd), lambda g: (l, 0, 0), **resident),
        pl.BlockSpec((9, Hd), const2),
        pl.BlockSpec((1, Hd), const2),
        pl.BlockSpec((None, Hd, D), lambda g: (l, 0, 0), **resident),
        pl.BlockSpec((1, tm, D), lambda g: (prev(g) // tpb, prev(g) % tpb, 0)),
        pl.BlockSpec((1, 1, D), mod_prev),
    ]
    args += [norm_w.reshape(1, D), shift, scale, w_up, conv_w9, conv_b.reshape(1, Hd), wd, x, gate]
    if final_w is not None:
        in_specs.append(pl.BlockSpec((1, D), const2))
        args.append(final_w.reshape(1, D))
    ext_rows = tm + 2 * grid_w if halo else tm
    return pl.pallas_call(
        functools.partial(_ffn_kernel, grid_w=grid_w, halo=halo, final_norm=final_w is not None,
                          tiles_per_batch=tpb, n_tiles=n_tiles),
        grid=(n_tiles + 1,),
        in_specs=in_specs,
        out_specs=pl.BlockSpec((1, tm, D), lambda g: (prev(g) // tpb, prev(g) % tpb, 0)),
        out_shape=jax.ShapeDtypeStruct((Bx, Tx, D), F32),
        scratch_shapes=[pltpu.VMEM((ext_rows, D), BF16), pltpu.VMEM((2, tm, Hd), BF16)],
        name="conv_ffn",
        compiler_params=_cparams(("arbitrary",)),
    )(*args)


def kernel(x, c, ctx, c_ctx, ada_w, ada_b, norm1_w, w_in, gla_lr_w, gla_lr_b, gla_norm_w, rnn_conv_w,
           rnn_conv_b, rnn_wa, rnn_ba, rnn_wx, rnn_bx, rnn_lambda, w_gla_o, w_rnn_o, w_out, norm2_w,
           ffn_up, ffn_conv_w, ffn_conv_b, ffn_down, final_norm_w):
    B, T, D = x.shape
    Tc = ctx.shape[1]
    depth = w_in.shape[0]

    cc = jnp.zeros((16, D), F32).at[:B].set(c).at[B].set(c_ctx)
    mod = _ada(cc, ada_w, ada_b)
    ctxf = ctx.reshape(1, B * Tc, D)

    lr_end = LR_START + 2 * GLA_LOWRANK
    w_z = jnp.concatenate([w_in[:, :, :LR_START], w_in[:, :, lr_end:]], axis=2).astype(BF16)
    w_lr = jnp.pad(w_in[:, :, LR_START:lr_end], ((0, 0), (0, 0), (0, LANES - 2 * GLA_LOWRANK))).astype(BF16)
    wg_o, wr_o, wo = w_gla_o.astype(BF16), w_rnn_o.astype(BF16), w_out.astype(BF16)
    w_up, w_dn = ffn_up.astype(BF16), (0.5 * ffn_down).astype(BF16)

    for l in range(depth):
        last = l == depth - 1
        mx = mod[l, :B].reshape(B, 1, 6 * D)
        mc = mod[l, B:B + 1].reshape(1, 1, 6 * D)
        sh1, sc1, g1, sh2, sc2, g2 = [mx[:, :, k * D:(k + 1) * D] for k in range(6)]
        csh1, csc1, cg1, csh2, csc2, cg2 = [mc[:, :, k * D:(k + 1) * D] for k in range(6)]

        lrw = jnp.zeros((2, LANES, GLA_KW), F32)
        lrw = lrw.at[0, :GLA_LOWRANK].set(gla_lr_w[l, 0]).at[1, GLA_LOWRANK:2 * GLA_LOWRANK].set(gla_lr_w[l, 1])
        lrw = lrw.astype(BF16)
        wgate = (0.5 * jnp.concatenate([rnn_wa[l, 0], rnn_wx[l, 0], rnn_wa[l, 1], rnn_wx[l, 1]], axis=-1)
                 ).astype(BF16)
        bgate = 0.5 * jnp.concatenate(
            [v.reshape(RNN_BLOCKS, 1, RNN_BLOCK) for v in (rnn_ba[l, 0], rnn_bx[l, 0], rnn_ba[l, 1], rnn_bx[l, 1])],
            axis=-1)
        conv9 = ffn_conv_w[l].reshape(9, FFN_HIDDEN)

        z, lr = _proj(x, norm1_w[l], sh1, sc1, w_z, w_lr, l, PROJ_TN_IN)
        zc, lrc = _proj(ctxf, norm1_w[l], csh1, csc1, w_z, w_lr, l, PROJ_TN_IN)
        zc = zc.reshape(B, Tc, Z_WIDTH)
        lrc = lrc.reshape(B, Tc, LANES)
        on, onc = _gla(z, lr, zc, lrc, lrw, gla_lr_b[l], gla_norm_w[l], ctx_out=not last)
        r, rc = _rglru(z, zc, rnn_conv_w[l], rnn_conv_b[l], wgate, bgate, rnn_lambda[l], ctx_out=not last)
        x = _mix(on, r, z, x, g1, wg_o, wr_o, wo, l)

        x = _ffn(x, norm2_w[l], sh2, sc2, g2, w_up, conv9, ffn_conv_b[l], w_dn, l, GRID_W,
                 final_norm_w if last else None)

        if not last:
            ctxf = _mix(onc.reshape(1, B * Tc, D), rc.reshape(1, B * Tc, D), zc.reshape(1, B * Tc, Z_WIDTH),
                        ctxf, cg1, wg_o, wr_o, wo, l)
            ctxf = _ffn(ctxf.reshape(B, Tc, D), norm2_w[l], csh2, csc2, cg2, w_up, conv9, ffn_conv_b[l],
                        w_dn, l, Tc, None).reshape(1, B * Tc, D)
    return x
```
